```python
import functools
import jax, jax.numpy as jnp
from jax import lax
import numpy as np

D_MODEL = 1024
BATCH = 4
SEQ = 4096
DEPTH = 4
DEC_BATCH = 128
DEC_SEQ = 1
PAST_LEN = 2048
PAGE_SIZE = 128

SB_HEADS = 8
SB_HD = 64
SB_W = SB_HEADS * SB_HD
SB_BLOCK = 128
ML_HEADS = 4
ML_HD = 128
ML_W = ML_HEADS * ML_HD
ML_CHUNK = 128
CV_W = 512
CV_K = 31
MEM_LEN = 256
XA_HEADS = 4
XA_HD = 128
XA_W = XA_HEADS * XA_HD
PK_HEADS = 8
PK_NKEYS = 128
PK_N = PK_NKEYS * PK_NKEYS
PK_DKEY = 256
PK_TOPK = 16
PK_BLOCK = 128
N_BRANCH = 3
EPS = 1e-6

IN_SIZES = (SB_W, SB_W, SB_W, ML_W, ML_W, ML_W, ML_W, ML_HEADS, ML_HEADS, 2 * CV_W, N_BRANCH * D_MODEL)
IN_DIM = sum(IN_SIZES)

kernel_name = 'hybrid_sb_mlstm_conv_peer_decode_step'


def _rmsnorm(x, g):
    xf = x.astype(jnp.float32)
    y = xf * lax.rsqrt(jnp.mean(xf * xf, axis=-1, keepdims=True) + EPS)
    return (y * g.astype(jnp.float32)).astype(x.dtype)


def _layernorm(x, g, b):
    xf = x.astype(jnp.float32)
    xc = xf - jnp.mean(xf, axis=-1, keepdims=True)
    y = xc * lax.rsqrt(jnp.mean(xc * xc, axis=-1, keepdims=True) + EPS)
    return (y * g.astype(jnp.float32) + b.astype(jnp.float32)).astype(x.dtype)


def _split_cols(z, sizes):
    parts, off = [], 0
    for s in sizes:
        parts.append(z[..., off:off + s])
        off += s
    return parts


def _sb_attend(q, k, v, bias, q_start):
    lq, lk = q.shape[1], k.shape[1]
    z = (jnp.einsum('bqhd,bkhd->bhqk', q, k).astype(jnp.float32) * (SB_HD ** -0.5)
         + bias.astype(jnp.float32)[None, :, None, None])
    causal = jnp.arange(lk)[None, :] < (q_start + jnp.arange(lq))[:, None]
    log_beta = jax.nn.log_sigmoid(z)
    log_1mb = jnp.where(causal, log_beta - z, 0.0)
    tail = lax.cumsum(log_1mb, axis=3, reverse=True) - log_1mb
    w = jnp.where(causal, jnp.exp(log_beta + tail), 0.0)
    return jnp.einsum('bhqk,bkhd->bqhd', w.astype(v.dtype), v)


def _sb_prompt(q, k, v, bias):
    seq = q.shape[1]
    outs = []
    for s in range(0, seq, SB_BLOCK):
        e = min(s + SB_BLOCK, seq)
        outs.append(_sb_attend(q[:, s:e], k[:, :e], v[:, :e], bias, s))
    return jnp.concatenate(outs, axis=1)


def _sb_cached(q, k, v, bias, past_k, past_v):
    kk = jnp.concatenate([past_k.astype(k.dtype), k], axis=1)
    vv = jnp.concatenate([past_v.astype(v.dtype), v], axis=1)
    return _sb_attend(q, kk, vv, bias, past_k.shape[1])


def _mlstm_chunk(carry, inp):
    C, n, m = carry
    q, k, v, ig, lf = inp
    L = q.shape[1]
    bcum = jnp.cumsum(jnp.swapaxes(lf, 1, 2), axis=-1)
    igh = jnp.swapaxes(ig, 1, 2)
    causal = jnp.tril(jnp.ones((L, L), dtype=bool))
    log_d = jnp.where(causal, bcum[..., :, None] - bcum[..., None, :] + igh[..., None, :], -jnp.inf)
    log_c = bcum + m[..., None]
    m_t = jnp.maximum(log_c, jnp.max(log_d, axis=-1))
    w_d = jnp.exp(log_d - m_t[..., None])
    w_c = jnp.exp(log_c - m_t)
    s = jnp.einsum('bthd,bshd->bhts', q, k) * w_d
    num = jnp.einsum('bhts,bshe->bhte', s, v) + w_c[..., None] * jnp.einsum('bthd,bhde->bhte', q, C)
    den = jnp.sum(s, axis=-1) + w_c * jnp.einsum('bthd,bhd->bht', q, n)
    h = num / jnp.maximum(jnp.abs(den), jnp.exp(-m_t))[..., None]
    b_end = bcum[..., -1]
    log_in = b_end[..., None] - bcum + igh
    m_new = jnp.maximum(b_end + m, jnp.max(log_in, axis=-1))
    w_in = jnp.exp(log_in - m_new[..., None])
    decay = jnp.exp(b_end + m - m_new)
    C_new = decay[..., None, None] * C + jnp.einsum('bhs,bshd,bshe->bhde', w_in, k, v)
    n_new = decay[..., None] * n + jnp.einsum('bhs,bshd->bhd', w_in, k)
    return (C_new, n_new, m_new), jnp.swapaxes(h, 1, 2)


def _mlstm_seq(q, k, v, ig, lf, C, n, m, chunk):
    B, L = q.shape[:2]
    nc = L // chunk
    to_chunks = lambda a: jnp.moveaxis(a.reshape((B, nc, chunk) + a.shape[2:]), 1, 0)
    (C, n, m), h = lax.scan(_mlstm_chunk, (C, n, m), tuple(to_chunks(a) for a in (q, k, v, ig, lf)))
    h = jnp.moveaxis(h, 0, 1).reshape((B, L) + h.shape[3:])
    return h, C, n, m


def _causal_dwconv(u, hist, w, b):
    xx = jnp.concatenate([hist.astype(u.dtype), u], axis=1)
    y = lax.conv_general_dilated(xx, w[:, None, :].astype(u.dtype), (1,), 'VALID',
                                 dimension_numbers=('NWC', 'WIO', 'NWC'), feature_group_count=CV_W)
    return y + b, xx[:, -(CV_K - 1):]


def _mem_kv(mem, g_mem, w_k, w_v, g_k):
    B, M, _ = mem.shape
    mn = _rmsnorm(mem, g_mem)
    k = _rmsnorm((mn @ w_k).reshape(B, M, XA_HEADS, XA_HD), g_k)
    v = (mn @ w_v).reshape(B, M, XA_HEADS, XA_HD)
    return k, v


def _mem_attend(xn, mem_k, mem_v, w_q, g_q, w_o):
    B, L, _ = xn.shape
    q = _rmsnorm((xn @ w_q).reshape(B, L, XA_HEADS, XA_HD), g_q)
    s = jnp.einsum('blhd,bmhd->bhlm', q, mem_k.astype(q.dtype)).astype(jnp.float32) * (XA_HD ** -0.5)
    p = jax.nn.softmax(s, axis=-1).astype(q.dtype)
    o = jnp.einsum('bhlm,bmhd->blhd', p, mem_v.astype(q.dtype)).reshape(B, L, XA_W)
    return o @ w_o


def _peer(xn, w_q, sub_k1, sub_k2, exp_u, exp_v):
    shape = xn.shape
    xt = xn.reshape(-1, D_MODEL)
    T = xt.shape[0]
    nb = -(-T // PK_BLOCK)
    xb_all = jnp.pad(xt, ((0, nb * PK_BLOCK - T), (0, 0))).reshape(nb, PK_BLOCK, D_MODEL)
    half = PK_DKEY // 2
    ncand = PK_TOPK * PK_TOPK

    def block(xb):
        q = (xb @ w_q).reshape(PK_BLOCK, PK_HEADS, 2, half)
        s1 = jnp.einsum('thd,hnd->thn', q[:, :, 0], sub_k1).astype(jnp.float32)
        s2 = jnp.einsum('thd,hnd->thn', q[:, :, 1], sub_k2).astype(jnp.float32)
        v1, i1 = lax.top_k(s1, PK_TOPK)
        v2, i2 = lax.top_k(s2, PK_TOPK)
        cand_s = (v1[..., :, None] + v2[..., None, :]).reshape(PK_BLOCK, PK_HEADS, ncand)
        cand_e = (i1[..., :, None] * PK_NKEYS + i2[..., None, :]).reshape(PK_BLOCK, PK_HEADS, ncand)
        top_s, pos = lax.top_k(cand_s, PK_TOPK)
        eid = jnp.take_along_axis(cand_e, pos, axis=-1)
        gate = jax.nn.softmax(top_s, axis=-1)
        act = jax.nn.gelu(jnp.einsum('td,thkd->thk', xb, exp_u[eid]).astype(jnp.float32), approximate=False)
        return jnp.einsum('thk,thkd->td', (gate * act).astype(xb.dtype), exp_v[eid])

    y = lax.map(block, xb_all).reshape(nb * PK_BLOCK, D_MODEL)[:T]
    return y.reshape(shape)


def _layer(x, lp, mem_k, mem_v, sb_mix, ml_state, conv_hist, ml_chunk):
    f32 = jnp.float32
    B, L, _ = x.shape
    xn = _rmsnorm(x, lp['g_mix'])
    z = xn @ lp['w_in']
    q_sb, k_sb, v_sb, q_ml, k_ml, v_ml, o_ml, i_ml, f_ml, glu, gates = _split_cols(z, IN_SIZES)
    q_sb = _rmsnorm(q_sb.reshape(B, L, SB_HEADS, SB_HD), lp['g_sb_q'])
    k_sb = _rmsnorm(k_sb.reshape(B, L, SB_HEADS, SB_HD), lp['g_sb_k'])
    v_sb = v_sb.reshape(B, L, SB_HEADS, SB_HD)
    h_sb = sb_mix(q_sb, k_sb, v_sb, lp['sb_bias']).reshape(B, L, SB_W)
    qm = q_ml.reshape(B, L, ML_HEADS, ML_HD).astype(f32)
    km = k_ml.reshape(B, L, ML_HEADS, ML_HD).astype(f32) * (ML_HD ** -0.5)
    vm = v_ml.reshape(B, L, ML_HEADS, ML_HD).astype(f32)
    ig = i_ml.astype(f32) + lp['ml_ibias'].astype(f32)
    lf = jax.nn.log_sigmoid(f_ml.astype(f32) + lp['ml_fbias'].astype(f32))
    C0, n0, m0 = ml_state
    hm, C1, n1, m1 = _mlstm_seq(qm, km, vm, ig, lf, C0, n0, m0, ml_chunk)
    h_ml = jax.nn.sigmoid(o_ml) * _rmsnorm(hm, lp['g_ml_h']).reshape(B, L, ML_W).astype(x.dtype)
    ga, gb = jnp.split(glu, 2, axis=-1)
    u = ga * jax.nn.sigmoid(gb)
    c, new_hist = _causal_dwconv(u, conv_hist, lp['cv_w'], lp['cv_b'])
    h_cv = jax.nn.silu(_layernorm(c, lp['cv_ln_g'], lp['cv_ln_b']))
    g = jax.nn.sigmoid(gates.reshape(B, L, N_BRANCH, D_MODEL))
    merged = (g[..., 0, :] * (h_sb @ lp['w_sb_o']) + g[..., 1, :] * (h_ml @ lp['w_ml_o'])
              + g[..., 2, :] * (h_cv @ lp['w_cv_o']))
    x = x + merged @ lp['w_out']
    x = x + _mem_attend(_rmsnorm(x, lp['g_xa']), mem_k, mem_v, lp['w_xq'], lp['g_xq'], lp['w_xo'])
    x = x + _peer(_rmsnorm(x, lp['g_ffn']), lp['w_pq'], lp['pk_k1'], lp['pk_k2'], lp['pk_u'], lp['pk_v'])
    return x, (k_sb, v_sb), (C1, n1, m1), new_hist


def setup_inputs(seed: int = 0) -> dict:
    key = jax.random.key(seed)
    ks = iter(jax.random.split(key, 64))
    f32 = jnp.float32
    nrm = lambda shape, scale: jax.random.normal(next(ks), shape, f32) * scale
    gain = lambda shape: 1.0 + nrm(shape, 0.02)
    L, D = DEPTH, D_MODEL
    n_pages = PAST_LEN // PAGE_SIZE
    n_pool = (DEC_BATCH * n_pages * 5) // 4
    x_prompt = nrm((BATCH, SEQ, D), 1.0)
    x_sample = nrm((DEC_BATCH, DEC_SEQ, D), 1.0)
    cache_sb_k = nrm((L, n_pool, PAGE_SIZE, SB_HEADS, SB_HD), 1.0)
    cache_sb_v = nrm((L, n_pool, PAGE_SIZE, SB_HEADS, SB_HD), 1.0)
    state_ml_C = nrm((L, DEC_BATCH, ML_HEADS, ML_HD, ML_HD), 0.1)
    state_ml_n = nrm((L, DEC_BATCH, ML_HEADS, ML_HD), 0.5)
    state_ml_m = nrm((L, DEC_BATCH, ML_HEADS), 0.5)
    state_conv = nrm((L, DEC_BATCH, CV_K - 1, CV_W), 0.5)
    cache_mem_k = nrm((L, DEC_BATCH, MEM_LEN, XA_HEADS, XA_HD), 1.0)
    cache_mem_v = nrm((L, DEC_BATCH, MEM_LEN, XA_HEADS, XA_HD), 1.0)
    page_table = jax.random.permutation(next(ks), n_pool)[:DEC_BATCH * n_pages].reshape(DEC_BATCH, n_pages).astype(jnp.int32)
    mem_prompt = nrm((BATCH, MEM_LEN, D), 1.0)
    return {
        'x_prompt': x_prompt, 'x_sample': x_sample,
        'cache_sb_k': cache_sb_k, 'cache_sb_v': cache_sb_v,
        'state_ml_C': state_ml_C, 'state_ml_n': state_ml_n, 'state_ml_m': state_ml_m,
        'state_conv': state_conv, 'cache_mem_k': cache_mem_k, 'cache_mem_v': cache_mem_v,
        'page_table': page_table, 'mem_prompt': mem_prompt,
        'g_mix': gain((L, D)),
        'w_in': nrm((L, D, IN_DIM), D ** -0.5),
        'g_sb_q': gain((L, SB_HD)), 'g_sb_k': gain((L, SB_HD)),
        'sb_bias': jnp.linspace(-7.0, -5.0, SB_HEADS, dtype=f32)[None, :] + nrm((L, SB_HEADS), 0.1),
        'ml_ibias': nrm((L, ML_HEADS), 0.1),
        'ml_fbias': jnp.linspace(3.0, 6.0, ML_HEADS, dtype=f32)[None, :] + nrm((L, ML_HEADS), 0.1),
        'g_ml_h': gain((L, ML_HEADS, ML_HD)),
        'cv_w': nrm((L, CV_K, CV_W), CV_K ** -0.5), 'cv_b': nrm((L, CV_W), 0.02),
        'cv_ln_g': gain((L, CV_W)), 'cv_ln_b': nrm((L, CV_W), 0.02),
        'w_sb_o': nrm((L, SB_W, D), SB_W ** -0.5), 'w_ml_o': nrm((L, ML_W, D), ML_W ** -0.5),
        'w_cv_o': nrm((L, CV_W, D), CV_W ** -0.5), 'w_out': nrm((L, D, D), D ** -0.5),
        'g_xa': gain((L, D)), 'g_mem': gain((L, D)),
        'w_xq': nrm((L, D, XA_W), D ** -0.5), 'w_xk': nrm((L, D, XA_W), D ** -0.5),
        'w_xv': nrm((L, D, XA_W), D ** -0.5),
        'g_xq': gain((L, XA_HD)), 'g_xk': gain((L, XA_HD)),
        'w_xo': nrm((L, XA_W, D), XA_W ** -0.5),
        'g_ffn': gain((L, D)),
        'w_pq': nrm((L, D, PK_HEADS * PK_DKEY), D ** -0.5),
        'pk_k1': nrm((L, PK_HEADS, PK_NKEYS, PK_DKEY // 2), (PK_DKEY // 2) ** -0.5),
        'pk_k2': nrm((L, PK_HEADS, PK_NKEYS, PK_DKEY // 2), (PK_DKEY // 2) ** -0.5),
        'pk_u': nrm((L, PK_N, D), D ** -0.5), 'pk_v': nrm((L, PK_N, D), D ** -0.5),
    }


def reference(x_prompt, x_sample, cache_sb_k, cache_sb_v, state_ml_C, state_ml_n, state_ml_m,
              state_conv, cache_mem_k, cache_mem_v, page_table, mem_prompt,
              g_mix, w_in, g_sb_q, g_sb_k, sb_bias, ml_ibias, ml_fbias, g_ml_h, cv_w, cv_b, cv_ln_g, cv_ln_b,
              w_sb_o, w_ml_o, w_cv_o, w_out, g_xa, g_mem, w_xq, w_xk, w_xv, g_xq, g_xk, w_xo,
              g_ffn, w_pq, pk_k1, pk_k2, pk_u, pk_v):
    f32 = jnp.float32
    bp, seq = x_prompt.shape[0], x_prompt.shape[1]
    bs, dec_seq = x_sample.shape[0], x_sample.shape[1]
    yp, ys = x_prompt, x_sample
    sbk_p, sbv_p, sbk_s, sbv_s = [], [], [], []
    mC_p, mn_p, mm_p, mC_s, mn_s, mm_s = [], [], [], [], [], []
    cv_p, cv_s, memk_p, memv_p = [], [], [], []
    for l in range(DEPTH):
        lp = {
            'g_mix': g_mix[l], 'w_in': w_in[l], 'g_sb_q': g_sb_q[l], 'g_sb_k': g_sb_k[l],
            'sb_bias': sb_bias[l],
            'ml_ibias': ml_ibias[l], 'ml_fbias': ml_fbias[l], 'g_ml_h': g_ml_h[l],
            'cv_w': cv_w[l], 'cv_b': cv_b[l], 'cv_ln_g': cv_ln_g[l], 'cv_ln_b': cv_ln_b[l],
            'w_sb_o': w_sb_o[l], 'w_ml_o': w_ml_o[l], 'w_cv_o': w_cv_o[l], 'w_out': w_out[l],
            'g_xa': g_xa[l], 'w_xq': w_xq[l], 'g_xq': g_xq[l], 'w_xo': w_xo[l],
            'g_ffn': g_ffn[l], 'w_pq': w_pq[l], 'pk_k1': pk_k1[l], 'pk_k2': pk_k2[l],
            'pk_u': pk_u[l], 'pk_v': pk_v[l],
        }
        mk, mv = _mem_kv(mem_prompt, g_mem[l], w_xk[l], w_xv[l], g_xk[l])
        st0 = (jnp.zeros((bp, ML_HEADS, ML_HD, ML_HD), f32), jnp.zeros((bp, ML_HEADS, ML_HD), f32),
               jnp.zeros((bp, ML_HEADS), f32))
        hist0 = jnp.zeros((bp, CV_K - 1, CV_W), x_prompt.dtype)
        yp, (kp, vp), (Cp, n_p, m_p), hp = _layer(yp, lp, mk, mv, _sb_prompt, st0, hist0, min(ML_CHUNK, seq))
        past_k = cache_sb_k[l][page_table].reshape(bs, -1, SB_HEADS, SB_HD)
        past_v = cache_sb_v[l][page_table].reshape(bs, -1, SB_HEADS, SB_HD)
        sb_mix = functools.partial(_sb_cached, past_k=past_k, past_v=past_v)
        st_s = (state_ml_C[l].astype(f32), state_ml_n[l].astype(f32), state_ml_m[l].astype(f32))
        ys, (k_s, v_s), (Cs, n_s, m_s), hs = _layer(ys, lp, cache_mem_k[l], cache_mem_v[l], sb_mix, st_s,
                                                    state_conv[l], dec_seq)
        sbk_p.append(kp); sbv_p.append(vp); sbk_s.append(k_s); sbv_s.append(v_s)
        mC_p.append(Cp); mn_p.append(n_p); mm_p.append(m_p)
        mC_s.append(Cs); mn_s.append(n_s); mm_s.append(m_s)
        cv_p.append(hp); cv_s.append(hs); memk_p.append(mk); memv_p.append(mv)
    return (yp, ys,
            jnp.stack(sbk_p), jnp.stack(sbv_p), jnp.stack(sbk_s), jnp.stack(sbv_s),
            jnp.stack(mC_p), jnp.stack(mn_p), jnp.stack(mm_p),
            jnp.stack(mC_s), jnp.stack(mn_s), jnp.stack(mm_s),
            jnp.stack(cv_p), jnp.stack(cv_s), jnp.stack(memk_p), jnp.stack(memv_p))
```

```python
import functools

import jax
import jax.numpy as jnp
from jax import lax
from jax.experimental import pallas as pl
from jax.experimental.pallas import tpu as pltpu

F32 = jnp.float32
BF16 = jnp.bfloat16
EPS = 1e-6
NEG_INF = float("-inf")

SB_HD = 64
ML_HD = 128
ML_CHUNK = 128
XA_HD = 128
CV_K = 31
PK_TOPK = 16
PK_NKEYS = 128
LANES = 128
VMEM_LIMIT = 56 * 1024 * 1024


def _params(*sem):
    return pltpu.CompilerParams(dimension_semantics=sem, vmem_limit_bytes=VMEM_LIMIT)


def _dot(a, b):
    return jnp.dot(a, b, preferred_element_type=F32)


def _dot_nt(a, b):
    return lax.dot_general(a, b, (((1,), (1,)), ((), ())), preferred_element_type=F32)


def _split2(x):
    hi = x.astype(BF16)
    lo = (x - hi.astype(F32)).astype(BF16)
    return hi, lo


def _split3(x):
    hi = x.astype(BF16)
    r = x - hi.astype(F32)
    mid = r.astype(BF16)
    lo = (r - mid.astype(F32)).astype(BF16)
    return hi, mid, lo


def _rms(x, g):
    ms = jnp.mean(x * x, axis=-1, keepdims=True)
    return x * lax.rsqrt(ms + EPS) * g


def _log_sigmoid(x):
    return jnp.minimum(x, 0.0) - jnp.log1p(jnp.exp(-jnp.abs(x)))


def _sigmoid(x):
    return 1.0 / (1.0 + jnp.exp(-x))


def _full(shape):
    n = len(shape)
    return pl.BlockSpec(shape, lambda *_: (0,) * n)


def _in_proj_kernel(x_ref, g_ref, wsb_ref, bd_ref, gq_ref, gk_ref, wml_ref,
                    wifh_ref, wifl_ref, bif_ref, wifth_ref, wiftl_ref, bift_ref,
                    wga_ref, wgb_ref,
                    q_ref, kf_ref, kb_ref, vf_ref, vb_ref, qm_ref, km_ref, vm_ref,
                    og_ref, gcol_ref, grow_ref, u_ref):
    x = x_ref[0]
    xn = _rms(x, g_ref[...])
    xh, xl = _split2(xn)

    sbw = q_ref.shape[-1]
    z = _dot(xh, wsb_ref[...])
    bd = bd_ref[...]

    def headnorm(zz, gain):
        sh, sl = _split2(zz * zz)
        ms = (_dot(sh, bd) + _dot(sl, bd)) * (1.0 / SB_HD)
        return zz * lax.rsqrt(ms + EPS) * gain

    qn = headnorm(z[:, :sbw], gq_ref[...])
    kn = headnorm(z[:, sbw:2 * sbw], gk_ref[...])
    vv = z[:, 2 * sbw:]
    q_ref[0] = (qn * (SB_HD ** -0.5)).astype(BF16)
    kf_ref[0] = kn
    kb_ref[0] = kn.astype(BF16)
    vf_ref[0] = vv
    vb_ref[0] = vv.astype(BF16)

    mlw = qm_ref.shape[-1]
    zm = _dot(xh, wml_ref[...])
    qm_ref[0] = zm[:, :mlw].astype(BF16)
    km_ref[0] = (zm[:, mlw:2 * mlw] * (ML_HD ** -0.5)).astype(BF16)
    vm_ref[0] = zm[:, 2 * mlw:3 * mlw].astype(BF16)
    og_ref[0] = _sigmoid(zm[:, 3 * mlw:])

    nh = grow_ref.shape[1] // 2
    wh, wl = wifh_ref[...], wifl_ref[...]
    zc = _dot(xh, wh) + _dot(xh, wl) + _dot(xl, wh) + bif_ref[...]
    lane = lax.broadcasted_iota(jnp.int32, zc.shape, 1)
    gcol_ref[0] = jnp.where(lane >= nh, _log_sigmoid(zc), zc)
    wth, wtl = wifth_ref[...], wiftl_ref[...]
    zr = _dot_nt(wth, xh) + _dot_nt(wtl, xh) + _dot_nt(wth, xl) + bift_ref[...]
    row = lax.broadcasted_iota(jnp.int32, zr.shape, 0)
    grow_ref[0] = jnp.where(row >= nh, _log_sigmoid(zr), zr)

    u_ref[0] = _dot(xh, wga_ref[...]) * _sigmoid(_dot(xh, wgb_ref[...]))


def _in_proj(x, lw):
    B, L, D = x.shape
    tm = min(L, 256)
    sbw = lw["w_sb"].shape[1] // 3
    mlw = lw["w_ml"].shape[1] // 4
    cvw = lw["w_ga"].shape[1]
    nh2 = lw["b_ift"].shape[0]
    tok = lambda c: pl.BlockSpec((1, tm, c), lambda b, i: (b, i, 0))
    weights = [lw["g_mix"], lw["w_sb"], lw["bd64"], lw["g_sb_q"], lw["g_sb_k"], lw["w_ml"],
               lw["w_if_h"], lw["w_if_l"], lw["b_if"], lw["w_ift_h"], lw["w_ift_l"], lw["b_ift"],
               lw["w_ga"], lw["w_gb"]]
    out_shape = [
        jax.ShapeDtypeStruct((B, L, sbw), BF16),
        jax.ShapeDtypeStruct((B, L, sbw), F32),
        jax.ShapeDtypeStruct((B, L, sbw), BF16),
        jax.ShapeDtypeStruct((B, L, sbw), F32),
        jax.ShapeDtypeStruct((B, L, sbw), BF16),
        jax.ShapeDtypeStruct((B, L, mlw), BF16),
        jax.ShapeDtypeStruct((B, L, mlw), BF16),
        jax.ShapeDtypeStruct((B, L, mlw), BF16),
        jax.ShapeDtypeStruct((B, L, mlw), F32),
        jax.ShapeDtypeStruct((B, L, LANES), F32),
        jax.ShapeDtypeStruct((B, nh2, L), F32),
        jax.ShapeDtypeStruct((B, L, cvw), F32),
    ]
    out_specs = [tok(sbw)] * 5 + [tok(mlw)] * 4 + [tok(LANES),
                 pl.BlockSpec((1, nh2, tm), lambda b, i: (b, 0, i)), tok(cvw)]
    return pl.pallas_call(
        _in_proj_kernel,
        grid=(B, L // tm),
        in_specs=[tok(D)] + [_full(w.shape) for w in weights],
        out_specs=out_specs,
        out_shape=out_shape,
        compiler_params=_params("parallel", "parallel"),
        name="in_proj",
    )(x, *weights)


def _sb_block(q, k, v, tri, c, bias, mask):
    z = _dot_nt(q, k) + bias
    l = jnp.log1p(jnp.exp(-jnp.abs(z)))
    lb = jnp.minimum(z, 0.0) - l
    l1 = lb - z
    if mask is not None:
        l1 = jnp.where(mask, l1, 0.0)
    hi, lo = _split2(l1)
    within = _dot(hi, tri) + _dot(lo, tri)
    w = jnp.exp(lb + within + c)
    if mask is not None:
        w = jnp.where(mask, w, 0.0)
    pv = _dot(w.astype(BF16), v)
    c_new = c + within[:, 0:1] + l1[:, 0:1]
    return pv, c_new


def _sb_prompt_kernel(bias_ref, q_ref, k_ref, v_ref, o_ref, *, tq):
    hp = pl.program_id(1)
    i = pl.program_id(2)
    q = q_ref[0]
    lane = lax.broadcasted_iota(jnp.int32, q.shape, 1)
    zero = jnp.zeros_like(q)
    qs = (jnp.where(lane < SB_HD, q, zero), jnp.where(lane >= SB_HD, q, zero))
    biases = (bias_ref[2 * hp], bias_ref[2 * hp + 1])
    row = lax.broadcasted_iota(jnp.int32, (tq, tq), 0)
    col = lax.broadcasted_iota(jnp.int32, (tq, tq), 1)
    tri = (row > col).astype(BF16)
    causal = col < row

    def step(j, carry, mask):
        start = pl.multiple_of(j * tq, tq)
        k = k_ref[0, pl.ds(start, tq), :]
        v = v_ref[0, pl.ds(start, tq), :]
        out = []
        for h in range(2):
            acc, c = carry[h]
            pv, c = _sb_block(qs[h], k, v, tri, c, biases[h], mask)
            out.append((acc + pv, c))
        return tuple(out)

    init = tuple((jnp.zeros((tq, 2 * SB_HD), F32), jnp.zeros((tq, 1), F32)) for _ in range(2))
    carry = step(i, init, causal)
    carry = lax.fori_loop(0, i, lambda t, cr: step(i - 1 - t, cr, None), carry)
    o_ref[0] = jnp.where(lane < SB_HD, carry[0][0], carry[1][0]).astype(o_ref.dtype)


def _sb_prompt(q, k, v, bias):
    B, L, W = q.shape
    tq = min(L, 256)
    pair = 2 * SB_HD
    return pl.pallas_call(
        functools.partial(_sb_prompt_kernel, tq=tq),
        grid_spec=pltpu.PrefetchScalarGridSpec(
            num_scalar_prefetch=1,
            grid=(B, W // pair, L // tq),
            in_specs=[pl.BlockSpec((1, tq, pair), lambda b, h, i, *_: (b, i, h)),
                      pl.BlockSpec((1, L, pair), lambda b, h, i, *_: (b, 0, h)),
                      pl.BlockSpec((1, L, pair), lambda b, h, i, *_: (b, 0, h))],
            out_specs=pl.BlockSpec((1, tq, pair), lambda b, h, i, *_: (b, i, h)),
        ),
        out_shape=jax.ShapeDtypeStruct((B, L, W), BF16),
        compiler_params=_params("parallel", "parallel", "parallel"),
        name="sb_prompt",
    )(bias, q, k, v)


def _sb_sample_kernel(pt_ref, qbd_ref, bias_ref, *refs, n_pages, page):
    k_refs = refs[:n_pages]
    v_refs = refs[n_pages:2 * n_pages]
    o_ref = refs[2 * n_pages]
    qbd = qbd_ref[0]
    nh, W = qbd.shape
    bias = bias_ref[...]
    row = lax.broadcasted_iota(jnp.int32, (page, page), 0)
    col = lax.broadcasted_iota(jnp.int32, (page, page), 1)
    tri = (row > col).astype(BF16)
    acc = jnp.zeros((nh, W), F32)
    c = jnp.zeros((nh, 1), F32)
    for p in reversed(range(n_pages)):
        k = k_refs[p][...].astype(BF16)
        v = v_refs[p][...].astype(BF16)
        pv, c = _sb_block(qbd, k, v, tri, c, bias, None)
        acc = acc + pv
    hrow = lax.broadcasted_iota(jnp.int32, (nh, W), 0)
    hlane = lax.broadcasted_iota(jnp.int32, (nh, W), 1) // SB_HD
    o_ref[0] = jnp.sum(jnp.where(hrow == hlane, acc, 0.0), axis=0, keepdims=True).astype(o_ref.dtype)


def _sb_sample(layer, page_table, qbd, bias_col, cache_k, cache_v):
    S, n_pages = page_table.shape
    _, _, page, W = cache_k.shape
    nh = qbd.shape[1]

    def page_spec(p):
        return pl.BlockSpec((None, None, page, W), lambda b, pt: (layer, pt[b, p], 0, 0))

    return pl.pallas_call(
        functools.partial(_sb_sample_kernel, n_pages=n_pages, page=page),
        grid_spec=pltpu.PrefetchScalarGridSpec(
            num_scalar_prefetch=1,
            grid=(S,),
            in_specs=[pl.BlockSpec((1, nh, W), lambda b, pt: (b, 0, 0)),
                      pl.BlockSpec((nh, 1), lambda b, pt: (0, 0))]
                     + [page_spec(p) for p in range(n_pages)] * 2,
            out_specs=pl.BlockSpec((1, 1, W), lambda b, pt: (b, 0, 0)),
        ),
        out_shape=jax.ShapeDtypeStruct((S, 1, W), BF16),
        compiler_params=_params("parallel"),
        name="sb_sample",
    )(page_table, qbd, bias_col, *([cache_k] * n_pages), *([cache_v] * n_pages))


def _mlstm_chunk_math(q, k, v, ig_col, ig_row, bcum_col, bcum_row, C, n, m):
    L = q.shape[0]
    row = lax.broadcasted_iota(jnp.int32, (L, L), 0)
    col = lax.broadcasted_iota(jnp.int32, (L, L), 1)
    log_d = jnp.where(col <= row, bcum_col - bcum_row + ig_row, NEG_INF)
    log_c = bcum_col + m
    m_t = jnp.maximum(log_c, jnp.max(log_d, axis=-1, keepdims=True))
    w_d = jnp.exp(log_d - m_t)
    w_c = jnp.exp(log_c - m_t)
    s = _dot_nt(q, k) * w_d
    num = _dot(s.astype(BF16), v) + w_c * _dot(q, C.astype(BF16))
    qf = q.astype(F32)
    den = jnp.sum(s, axis=-1, keepdims=True) + w_c * jnp.sum(qf * n, axis=-1, keepdims=True)
    h = num / jnp.maximum(jnp.abs(den), jnp.exp(-m_t))
    b_end = bcum_col[L - 1:L, :]
    log_in = b_end - bcum_col + ig_col
    m_new = jnp.maximum(b_end + m, jnp.max(log_in, axis=0, keepdims=True))
    w_in = jnp.exp(log_in - m_new)
    decay = jnp.exp(b_end + m - m_new)
    kw = k.astype(F32) * w_in
    C_new = decay * C + _dot(kw.T.astype(BF16), v)
    n_new = decay * n + jnp.sum(kw, axis=0, keepdims=True)
    return h, C_new, n_new, m_new


def _mlstm_prompt_kernel(q_ref, k_ref, v_ref, og_ref, gcol_ref, grow_ref, gh_ref,
                         h_ref, c_out, n_out, m_out, c_scr, n_scr, m_scr):
    t = pl.program_id(1)
    nh = c_scr.shape[0]
    L = q_ref.shape[1]

    @pl.when(t == 0)
    def _():
        c_scr[...] = jnp.zeros_like(c_scr)
        n_scr[...] = jnp.zeros_like(n_scr)
        m_scr[...] = jnp.zeros_like(m_scr)

    row = lax.broadcasted_iota(jnp.int32, (L, L), 0)
    col = lax.broadcasted_iota(jnp.int32, (L, L), 1)
    lower = (col <= row).astype(BF16)
    upper = (row <= col).astype(BF16)
    gcol = gcol_ref[0]
    grow = grow_ref[0]
    cum_col = sum(_dot(lower, part) for part in _split3(gcol))
    cum_row = sum(_dot(part, upper) for part in _split3(grow))
    for h in range(nh):
        sl = slice(h * ML_HD, (h + 1) * ML_HD)
        q, k, v = q_ref[0, :, sl], k_ref[0, :, sl], v_ref[0, :, sl]
        hh, C, n, m = _mlstm_chunk_math(
            q, k, v, gcol[:, h:h + 1], grow[h:h + 1, :],
            cum_col[:, nh + h:nh + h + 1], cum_row[nh + h:nh + h + 1, :],
            c_scr[h], n_scr[h], m_scr[h])
        c_scr[h] = C
        n_scr[h] = n
        m_scr[h] = m
        hn = _rms(hh, gh_ref[h:h + 1, :])
        h_ref[0, :, sl] = (og_ref[0, :, sl] * hn).astype(h_ref.dtype)

    @pl.when(t == pl.num_programs(1) - 1)
    def _():
        c_out[0] = c_scr[...]
        n_out[0] = n_scr[...]
        m_out[0] = m_scr[...]


def _mlstm_prompt(qm, km, vm, og, gcol, grow, g_h):
    B, L, W = qm.shape
    nh = W // ML_HD
    ch = min(L, ML_CHUNK)
    tok = lambda c: pl.BlockSpec((1, ch, c), lambda b, t: (b, t, 0))
    return pl.pallas_call(
        _mlstm_prompt_kernel,
        grid=(B, L // ch),
        in_specs=[tok(W), tok(W), tok(W), tok(W), tok(gcol.shape[-1]),
                  pl.BlockSpec((1, 2 * nh, ch), lambda b, t: (b, 0, t)),
                  _full(g_h.shape)],
        out_specs=[tok(W),
                   pl.BlockSpec((1, nh, ML_HD, ML_HD), lambda b, t: (b, 0, 0, 0)),
                   pl.BlockSpec((1, nh, 1, ML_HD), lambda b, t: (b, 0, 0, 0)),
                   pl.BlockSpec((1, nh, 1, 1), lambda b, t: (b, 0, 0, 0))],
        out_shape=[jax.ShapeDtypeStruct((B, L, W), BF16),
                   jax.ShapeDtypeStruct((B, nh, ML_HD, ML_HD), F32),
                   jax.ShapeDtypeStruct((B, nh, 1, ML_HD), F32),
                   jax.ShapeDtypeStruct((B, nh, 1, 1), F32)],
        scratch_shapes=[pltpu.VMEM((nh, ML_HD, ML_HD), F32),
                        pltpu.VMEM((nh, 1, ML_HD), F32),
                        pltpu.VMEM((nh, 1, 1), F32)],
        compiler_params=_params("parallel", "arbitrary"),
        name="mlstm_prompt",
    )(qm, km, vm, og, gcol, grow, g_h)


def _mlstm_sample_kernel(q_ref, k_ref, v_ref, og_ref, g_ref, gh_ref, c_ref, n_ref, m_ref,
                         h_ref, c_out, n_out, m_out):
    nh = c_ref.shape[1]
    d = c_ref.shape[-1]
    eye = (lax.broadcasted_iota(jnp.int32, (d, d), 0) == lax.broadcasted_iota(jnp.int32, (d, d), 1))

    def to_col(r):
        return jnp.sum(jnp.where(eye, r, 0.0), axis=1, keepdims=True)

    g = g_ref[0]
    for h in range(nh):
        sl = slice(h * d, (h + 1) * d)
        q = q_ref[0, :, sl].astype(F32)
        k = k_ref[0, :, sl].astype(F32)
        v = v_ref[0, :, sl].astype(F32)
        C = c_ref[0, h]
        n = n_ref[0, h]
        m = m_ref[0, h]
        ig = g[:, h:h + 1]
        lf = g[:, nh + h:nh + h + 1]
        log_c = lf + m
        m_t = jnp.maximum(log_c, ig)
        w_d = jnp.exp(ig - m_t)
        w_c = jnp.exp(log_c - m_t)
        s = jnp.sum(q * k, axis=-1, keepdims=True) * w_d
        cb = C.astype(BF16).astype(F32)
        qC = jnp.sum(to_col(q) * cb, axis=0, keepdims=True)
        num = s * v + w_c * qC
        den = s + w_c * jnp.sum(q * n, axis=-1, keepdims=True)
        hh = num / jnp.maximum(jnp.abs(den), jnp.exp(-m_t))
        c_out[0, h] = w_c * C + w_d * (to_col(k) * v)
        n_out[0, h] = w_c * n + w_d * k
        m_out[0, h] = m_t
        hn = _rms(hh, gh_ref[h:h + 1, :])
        h_ref[0, :, sl] = (og_ref[0, :, sl] * hn).astype(h_ref.dtype)


def _mlstm_sample(qm, km, vm, og, gcol, g_h, C, n, m):
    S, _, W = qm.shape
    nh = C.shape[1]
    d = C.shape[-1]
    seq = lambda *tail: pl.BlockSpec((1,) + tail, lambda b: (b,) + (0,) * len(tail))
    return pl.pallas_call(
        _mlstm_sample_kernel,
        grid=(S,),
        in_specs=[seq(1, W), seq(1, W), seq(1, W), seq(1, W), seq(1, gcol.shape[-1]), _full(g_h.shape),
                  seq(nh, d, d), seq(nh, 1, d), seq(nh, 1, 1)],
        out_specs=[seq(1, W), seq(nh, d, d), seq(nh, 1, d), seq(nh, 1, 1)],
        out_shape=[jax.ShapeDtypeStruct((S, 1, W), BF16),
                   jax.ShapeDtypeStruct(C.shape, F32),
                   jax.ShapeDtypeStruct(n.shape, F32),
                   jax.ShapeDtypeStruct(m.shape, F32)],
        compiler_params=_params("parallel"),
        name="mlstm_sample",
    )(qm, km, vm, og, gcol, g_h, C, n, m)


def _ln_swish(c, g, b):
    cc = c - jnp.mean(c, axis=-1, keepdims=True)
    y = cc * lax.rsqrt(jnp.mean(cc * cc, axis=-1, keepdims=True) + EPS) * g + b
    return y * _sigmoid(y)


CV_HALO = 32


def _conv_prompt_kernel(u_ref, halo_ref, w_ref, b_ref, g_ref, beta_ref, o_ref, win_ref):
    i = pl.program_id(1)
    tl = u_ref.shape[1]
    halo = halo_ref[0]
    win_ref[0:CV_HALO, :] = jnp.where(i == 0, jnp.zeros_like(halo), halo)
    win_ref[CV_HALO:, :] = u_ref[0]
    off = CV_HALO - (CV_K - 1)
    acc = jnp.zeros((tl, u_ref.shape[2]), F32) + b_ref[...]
    for j in range(CV_K):
        acc = acc + w_ref[j:j + 1, :] * win_ref[off + j:off + j + tl, :]
    o_ref[0] = _ln_swish(acc, g_ref[...], beta_ref[...]).astype(o_ref.dtype)


def _conv_prompt(u, w, b, g, beta):
    B, L, C = u.shape
    tl = min(L, 256)
    r = tl // CV_HALO
    return pl.pallas_call(
        _conv_prompt_kernel,
        grid=(B, L // tl),
        in_specs=[pl.BlockSpec((1, tl, C), lambda b_, i: (b_, i, 0)),
                  pl.BlockSpec((1, CV_HALO, C), lambda b_, i: (b_, jnp.maximum(i * r - 1, 0), 0)),
                  _full(w.shape), _full(b.shape), _full(g.shape), _full(beta.shape)],
        out_specs=pl.BlockSpec((1, tl, C), lambda b_, i: (b_, i, 0)),
        out_shape=jax.ShapeDtypeStruct((B, L, C), BF16),
        scratch_shapes=[pltpu.VMEM((tl + CV_HALO, C), F32)],
        compiler_params=_params("parallel", "parallel"),
        name="conv_prompt",
    )(u, u, w, b, g, beta)


def _conv_sample_kernel(xx_ref, w_ref, b_ref, g_ref, beta_ref, o_ref):
    xx = xx_ref[...]
    c = jnp.sum(xx * w_ref[...][None], axis=1) + b_ref[...]
    o_ref[...] = _ln_swish(c, g_ref[...], beta_ref[...]).astype(o_ref.dtype)


def _conv_sample(xx, w, b, g, beta):
    S, K, C = xx.shape
    nb = min(S, 8)
    return pl.pallas_call(
        _conv_sample_kernel,
        grid=(S // nb,),
        in_specs=[pl.BlockSpec((nb, K, C), lambda s: (s, 0, 0)),
                  _full(w.shape), _full(b.shape), _full(g.shape), _full(beta.shape)],
        out_specs=pl.BlockSpec((nb, C), lambda s: (s, 0)),
        out_shape=jax.ShapeDtypeStruct((S, C), BF16),
        compiler_params=_params("parallel"),
        name="conv_sample",
    )(xx, w, b, g, beta)


def _merge_kernel(x_ref, g_ref, wg_ref, hs_ref, hm_ref, hc_ref, ws_ref, wm_ref, wc_ref, wo_ref, o_ref):
    x = x_ref[...]
    D = x.shape[-1]
    xn = _rms(x, g_ref[...]).astype(BF16)
    merged = None
    for b, (h_ref, w_ref) in enumerate(((hs_ref, ws_ref), (hm_ref, wm_ref), (hc_ref, wc_ref))):
        gate = _sigmoid(_dot(xn, wg_ref[:, b * D:(b + 1) * D]))
        term = gate * _dot(h_ref[...], w_ref[...])
        merged = term if merged is None else merged + term
    o_ref[...] = x + _dot(merged.astype(BF16), wo_ref[...])


def _merge(x, h_sb, h_ml, h_cv, lw):
    T, D = x.shape
    tm = min(T, 256)
    W = h_sb.shape[1]
    tok = lambda c: pl.BlockSpec((tm, c), lambda i: (i, 0))
    weights = [lw["w_sb_o"], lw["w_ml_o"], lw["w_cv_o"], lw["w_out"]]
    return pl.pallas_call(
        _merge_kernel,
        grid=(T // tm,),
        in_specs=[tok(D), _full(lw["g_mix"].shape), _full(lw["w_gates"].shape), tok(W), tok(W), tok(W)]
                 + [_full(w.shape) for w in weights],
        out_specs=tok(D),
        out_shape=jax.ShapeDtypeStruct((T, D), F32),
        compiler_params=_params("parallel"),
        name="merge",
    )(x, lw["g_mix"], lw["w_gates"], h_sb, h_ml, h_cv, *weights)


def _norm_proj_kernel(x_ref, g_ref, w_ref, gh_ref, o_ref, *, head_norm):
    xn = _rms(x_ref[...], g_ref[...]).astype(BF16)
    z = _dot(xn, w_ref[...])
    if head_norm:
        for h in range(z.shape[1] // XA_HD):
            sl = slice(h * XA_HD, (h + 1) * XA_HD)
            o_ref[:, sl] = _rms(z[:, sl], gh_ref[...]).astype(o_ref.dtype)
    else:
        o_ref[...] = z.astype(o_ref.dtype)


def _norm_proj(x, g, w, gh, head_norm, out_dtype):
    T, D = x.shape
    N = w.shape[1]
    tm = min(T, 256)
    return pl.pallas_call(
        functools.partial(_norm_proj_kernel, head_norm=head_norm),
        grid=(T // tm,),
        in_specs=[pl.BlockSpec((tm, D), lambda i: (i, 0)), _full(g.shape), _full(w.shape), _full(gh.shape)],
        out_specs=pl.BlockSpec((tm, N), lambda i: (i, 0)),
        out_shape=jax.ShapeDtypeStruct((T, N), out_dtype),
        compiler_params=_params("parallel"),
        name="norm_proj",
    )(x, g, w, gh)


def _proj_res_kernel(x_ref, h_ref, w_ref, o_ref):
    o_ref[...] = x_ref[...] + _dot(h_ref[...], w_ref[...])


def _proj_res(x, h, w):
    T, D = x.shape
    tm = min(T, 256)
    return pl.pallas_call(
        _proj_res_kernel,
        grid=(T // tm,),
        in_specs=[pl.BlockSpec((tm, D), lambda i: (i, 0)),
                  pl.BlockSpec((tm, h.shape[1]), lambda i: (i, 0)), _full(w.shape)],
        out_specs=pl.BlockSpec((tm, D), lambda i: (i, 0)),
        out_shape=jax.ShapeDtypeStruct((T, D), F32),
        compiler_params=_params("parallel"),
        name="proj_res",
    )(x, h, w)


def _xa_prompt_kernel(q_ref, k_ref, v_ref, o_ref):
    k = k_ref[0].astype(BF16)
    v = v_ref[0].astype(BF16)
    for h in range(q_ref.shape[-1] // XA_HD):
        sl = slice(h * XA_HD, (h + 1) * XA_HD)
        s = _dot_nt(q_ref[0, :, sl], k[:, sl]) * (XA_HD ** -0.5)
        e = jnp.exp(s - jnp.max(s, axis=-1, keepdims=True))
        p = e / jnp.sum(e, axis=-1, keepdims=True)
        o_ref[0, :, sl] = _dot(p.astype(BF16), v[:, sl]).astype(o_ref.dtype)


def _xa_prompt(q, mk, mv):
    B, L, W = q.shape
    M = mk.shape[1]
    tm = min(L, 512)
    return pl.pallas_call(
        _xa_prompt_kernel,
        grid=(B, L // tm),
        in_specs=[pl.BlockSpec((1, tm, W), lambda b, i: (b, i, 0)),
                  pl.BlockSpec((1, M, W), lambda b, i: (b, 0, 0)),
                  pl.BlockSpec((1, M, W), lambda b, i: (b, 0, 0))],
        out_specs=pl.BlockSpec((1, tm, W), lambda b, i: (b, i, 0)),
        out_shape=jax.ShapeDtypeStruct((B, L, W), BF16),
        compiler_params=_params("parallel", "parallel"),
        name="xa_prompt",
    )(q, mk, mv)


def _xa_sample_kernel(q_ref, k_ref, v_ref, o_ref):
    q = q_ref[0].astype(F32)
    k = k_ref[...].astype(BF16).astype(F32)
    v = v_ref[...].astype(BF16).astype(F32)
    prod = k * q
    for h in range(q.shape[-1] // XA_HD):
        sl = slice(h * XA_HD, (h + 1) * XA_HD)
        s = jnp.sum(prod[:, sl], axis=-1, keepdims=True) * (XA_HD ** -0.5)
        e = jnp.exp(s - jnp.max(s, axis=0, keepdims=True))
        p = e / jnp.sum(e, axis=0, keepdims=True)
        p = p.astype(BF16).astype(F32)
        o_ref[0, :, sl] = jnp.sum(p * v[:, sl], axis=0, keepdims=True).astype(o_ref.dtype)


def _xa_sample(layer, q, cache_k, cache_v):
    S, _, W = q.shape
    M = cache_k.shape[2]
    mem = pl.BlockSpec((None, None, M, W), lambda b: (layer, b, 0, 0))
    return pl.pallas_call(
        _xa_sample_kernel,
        grid=(S,),
        in_specs=[pl.BlockSpec((1, 1, W), lambda b: (b, 0, 0)), mem, mem],
        out_specs=pl.BlockSpec((1, 1, W), lambda b: (b, 0, 0)),
        out_shape=jax.ShapeDtypeStruct((S, 1, W), BF16),
        compiler_params=_params("parallel"),
        name="xa_sample",
    )(q, cache_k, cache_v)


def _take_top(s, n):
    idx = lax.broadcasted_iota(jnp.int32, s.shape, 0)
    big = jnp.int32(s.shape[0])
    slot = lax.broadcasted_iota(jnp.int32, (n, s.shape[1]), 0)
    vals = jnp.zeros((n, s.shape[1]), F32)
    for r in range(n):
        mx = jnp.max(s, axis=0, keepdims=True)
        first = jnp.min(jnp.where(s == mx, idx, big), axis=0, keepdims=True)
        s = jnp.where(idx == first, NEG_INF, s)
        vals = jnp.where(slot == r, mx, vals)
    return vals, s


def _cand_pieces(v1, v2):
    sub = lax.broadcasted_iota(jnp.int32, (8, v1.shape[1]), 0)
    pieces = [v1[0:1] + v2]
    for i in range(1, 8):
        nj = PK_TOPK // (i + 1)
        pieces.append(jnp.where(sub < nj, v1[i:i + 1] + v2[0:8], NEG_INF))
    pieces.append(v1[8:16] + v2[0:1])
    return jnp.concatenate(pieces, axis=0)


def _peer_topk_kernel(x_ref, g_ref, wq_ref, k1h_ref, k1l_ref, k2h_ref, k2l_ref,
                      xt_ref, s1_ref, s2_ref, e1_ref, e2_ref, tau_ref):
    xn = _rms(x_ref[...], g_ref[...])
    xt_ref[...] = xn.T.astype(BF16)
    q = _dot(xn.astype(BF16), wq_ref[...])
    nheads = k1h_ref.shape[0]
    half = k1h_ref.shape[2]
    for h in range(nheads):
        q1h, q1l = _split2(q[:, (2 * h) * half:(2 * h + 1) * half])
        q2h, q2l = _split2(q[:, (2 * h + 1) * half:(2 * h + 2) * half])
        s1 = _dot_nt(k1h_ref[h], q1h) + _dot_nt(k1l_ref[h], q1h) + _dot_nt(k1h_ref[h], q1l)
        s2 = _dot_nt(k2h_ref[h], q2h) + _dot_nt(k2l_ref[h], q2h) + _dot_nt(k2h_ref[h], q2l)
        v1, r1 = _take_top(s1, PK_TOPK)
        v2, r2 = _take_top(s2, PK_TOPK)
        top, _ = _take_top(_cand_pieces(v1, v2), PK_TOPK)
        tau = top[PK_TOPK - 1:PK_TOPK]
        zsum = jnp.sum(jnp.exp(top - top[0:1]), axis=0, keepdims=True)
        s1_ref[h] = s1
        s2_ref[h] = s2
        e1_ref[h] = jnp.where(r1 == NEG_INF, jnp.exp(s1 - v1[0:1]) / zsum, 0.0)
        e2_ref[h] = jnp.where(r2 == NEG_INF, jnp.exp(s2 - v2[0:1]), 0.0)
        tau_ref[h:h + 1, :] = tau


def _peer_topk(x, lw):
    T, D = x.shape
    tm = min(T, 256)
    nheads, nkeys, _ = lw["pk_k1_h"].shape
    weights = [lw["g_ffn"], lw["w_pq"], lw["pk_k1_h"], lw["pk_k1_l"], lw["pk_k2_h"], lw["pk_k2_l"]]
    key_arr = jax.ShapeDtypeStruct((nheads, nkeys, T), F32)
    key_spec = pl.BlockSpec((nheads, nkeys, tm), lambda i: (0, 0, i))
    return pl.pallas_call(
        _peer_topk_kernel,
        grid=(T // tm,),
        in_specs=[pl.BlockSpec((tm, D), lambda i: (i, 0))] + [_full(w.shape) for w in weights],
        out_specs=[pl.BlockSpec((D, tm), lambda i: (0, i))] + [key_spec] * 4
                  + [pl.BlockSpec((nheads, tm), lambda i: (0, i))],
        out_shape=[jax.ShapeDtypeStruct((D, T), BF16)] + [key_arr] * 4
                  + [jax.ShapeDtypeStruct((nheads, T), F32)],
        compiler_params=_params("parallel"),
        name="peer_topk",
    )(x, *weights)


def _gelu(x):
    return 0.5 * x * (1.0 + lax.erf(x * (2.0 ** -0.5)))


def _peer_dense_kernel(x_ref, xt_ref, u_ref, vt_ref, s1_ref, s2_ref, e1_ref, e2_ref, tau_ref,
                       o_ref, acc_ref, act_ref, w_ref):
    j = pl.program_id(1)
    nheads, nkeys, tm = s2_ref.shape
    na = s1_ref.shape[1]

    @pl.when(j == 0)
    def _():
        acc_ref[...] = jnp.zeros_like(acc_ref)

    act_ref[...] = _gelu(_dot(u_ref[...], xt_ref[...]))
    cw = min(tm, LANES)
    for c in range(tm // cw):
        ln = slice(c * cw, (c + 1) * cw)
        for al in range(na):
            g = jnp.zeros((nkeys, cw), F32)
            for h in range(nheads):
                s = s1_ref[h, al:al + 1, ln] + s2_ref[h, :, ln]
                g = g + jnp.where(s >= tau_ref[h:h + 1, ln], e1_ref[h, al:al + 1, ln] * e2_ref[h, :, ln], 0.0)
            rows = slice(al * nkeys, (al + 1) * nkeys)
            w_ref[rows, ln] = (g * act_ref[rows, ln]).astype(BF16)
    acc_ref[...] += _dot(vt_ref[...], w_ref[...])

    @pl.when(j == pl.num_programs(1) - 1)
    def _():
        o_ref[...] = x_ref[...] + acc_ref[...].T


def _peer_dense(x, xt, s1, s2, e1, e2, tau, lw):
    T, D = x.shape
    u, vt = lw["pk_u"], lw["pk_vt"]
    E = u.shape[0]
    nheads, nkeys, _ = s1.shape
    tm = min(T, 512)
    na = 8
    te = na * nkeys
    key_spec = pl.BlockSpec((nheads, nkeys, tm), lambda i, j: (0, 0, i))
    a_spec = pl.BlockSpec((nheads, na, tm), lambda i, j: (0, j, i))
    return pl.pallas_call(
        _peer_dense_kernel,
        grid=(T // tm, E // te),
        in_specs=[pl.BlockSpec((tm, D), lambda i, j: (i, 0)),
                  pl.BlockSpec((D, tm), lambda i, j: (0, i)),
                  pl.BlockSpec((te, D), lambda i, j: (j, 0)),
                  pl.BlockSpec((D, te), lambda i, j: (0, j)),
                  a_spec, key_spec, a_spec, key_spec,
                  pl.BlockSpec((nheads, tm), lambda i, j: (0, i))],
        out_specs=pl.BlockSpec((tm, D), lambda i, j: (i, 0)),
        out_shape=jax.ShapeDtypeStruct((T, D), F32),
        scratch_shapes=[pltpu.VMEM((D, tm), F32), pltpu.VMEM((te, tm), F32), pltpu.VMEM((te, tm), BF16)],
        compiler_params=_params("parallel", "arbitrary"),
        name="peer_dense",
    )(x, xt, u, vt, s1, s2, e1, e2, tau)


def _prep_weights(l, p):
    w_in = p["w_in"][l]
    D = w_in.shape[0]
    sbw = p["w_sb_o"].shape[1]
    mlw = p["w_ml_o"].shape[1]
    cvw = p["w_cv_o"].shape[1]
    nh = p["ml_ibias"].shape[1]
    o = 0
    w_sb = w_in[:, o:o + 3 * sbw]; o += 3 * sbw
    w_ml = w_in[:, o:o + 4 * mlw]; o += 4 * mlw
    w_if = w_in[:, o:o + 2 * nh]; o += 2 * nh
    w_ga = w_in[:, o:o + cvw]; o += cvw
    w_gb = w_in[:, o:o + cvw]; o += cvw
    w_gates = w_in[:, o:]
    w_if_pad = jnp.pad(w_if, ((0, 0), (0, LANES - 2 * nh)))
    w_if_h, w_if_l = _split2(w_if_pad)
    w_ift_h, w_ift_l = _split2(w_if.T)
    b_if = jnp.concatenate([p["ml_ibias"][l], p["ml_fbias"][l]])
    idx = jnp.arange(sbw) // SB_HD
    k1h, k1l = _split2(p["pk_k1"][l])
    k2h, k2l = _split2(p["pk_k2"][l])
    row = lambda a: a.reshape(1, -1)
    return {
        "g_mix": row(p["g_mix"][l]), "w_sb": w_sb.astype(BF16), "w_ml": w_ml.astype(BF16),
        "bd64": (idx[:, None] == idx[None, :]).astype(BF16),
        "g_sb_q": row(jnp.tile(p["g_sb_q"][l], sbw // SB_HD)),
        "g_sb_k": row(jnp.tile(p["g_sb_k"][l], sbw // SB_HD)),
        "w_if_h": w_if_h, "w_if_l": w_if_l, "b_if": row(jnp.pad(b_if, (0, LANES - 2 * nh))),
        "w_ift_h": w_ift_h, "w_ift_l": w_ift_l, "b_ift": b_if.reshape(-1, 1),
        "w_ga": w_ga.astype(BF16), "w_gb": w_gb.astype(BF16), "w_gates": w_gates.astype(BF16),
        "sb_bias": p["sb_bias"][l], "g_ml_h": p["g_ml_h"][l],
        "cv_w": p["cv_w"][l], "cv_b": row(p["cv_b"][l]),
        "cv_ln_g": row(p["cv_ln_g"][l]), "cv_ln_b": row(p["cv_ln_b"][l]),
        "w_sb_o": p["w_sb_o"][l].astype(BF16), "w_ml_o": p["w_ml_o"][l].astype(BF16),
        "w_cv_o": p["w_cv_o"][l].astype(BF16), "w_out": p["w_out"][l].astype(BF16),
        "g_xa": row(p["g_xa"][l]), "g_mem": row(p["g_mem"][l]),
        "w_xq": p["w_xq"][l].astype(BF16), "w_xk": p["w_xk"][l].astype(BF16),
        "w_xv": p["w_xv"][l].astype(BF16), "g_xq": row(p["g_xq"][l]), "g_xk": row(p["g_xk"][l]),
        "w_xo": p["w_xo"][l].astype(BF16),
        "g_ffn": row(p["g_ffn"][l]), "w_pq": p["w_pq"][l].astype(BF16),
        "pk_k1_h": k1h, "pk_k1_l": k1l, "pk_k2_h": k2h, "pk_k2_l": k2l,
        "pk_u": p["pk_u"][l].astype(BF16), "pk_vt": p["pk_v"][l].T.astype(BF16),
    }


def _tail(x, h_sb, h_ml, h_cv, lw, xa_core):
    x = _merge(x, h_sb, h_ml, h_cv, lw)
    q = _norm_proj(x, lw["g_xa"], lw["w_xq"], lw["g_xq"], True, BF16)
    x = _proj_res(x, xa_core(q), lw["w_xo"])
    xt, s1, s2, e1, e2, tau = _peer_topk(x, lw)
    return _peer_dense(x, xt, s1, s2, e1, e2, tau, lw)


def kernel(x_prompt, x_sample, cache_sb_k, cache_sb_v, state_ml_C, state_ml_n, state_ml_m, state_conv, cache_mem_k, cache_mem_v, page_table, mem_prompt, g_mix, w_in, g_sb_q, g_sb_k, sb_bias, ml_ibias, ml_fbias, g_ml_h, cv_w, cv_b, cv_ln_g, cv_ln_b, w_sb_o, w_ml_o, w_cv_o, w_out, g_xa, g_mem, w_xq, w_xk, w_xv, g_xq, g_xk, w_xo, g_ffn, w_pq, pk_k1, pk_k2, pk_u, pk_v):
    p = dict(g_mix=g_mix, w_in=w_in, g_sb_q=g_sb_q, g_sb_k=g_sb_k, sb_bias=sb_bias, ml_ibias=ml_ibias,
             ml_fbias=ml_fbias, g_ml_h=g_ml_h, cv_w=cv_w, cv_b=cv_b, cv_ln_g=cv_ln_g, cv_ln_b=cv_ln_b,
             w_sb_o=w_sb_o, w_ml_o=w_ml_o, w_cv_o=w_cv_o, w_out=w_out, g_xa=g_xa, g_mem=g_mem, w_xq=w_xq,
             w_xk=w_xk, w_xv=w_xv, g_xq=g_xq, g_xk=g_xk, w_xo=w_xo, g_ffn=g_ffn, w_pq=w_pq,
             pk_k1=pk_k1, pk_k2=pk_k2, pk_u=pk_u, pk_v=pk_v)
    depth = w_in.shape[0]
    B, L, D = x_prompt.shape
    S = x_sample.shape[0]
    nh_sb = sb_bias.shape[1]
    nh_ml = ml_ibias.shape[1]
    sbw = w_sb_o.shape[1]
    M = mem_prompt.shape[1]
    xaw = w_xo.shape[1]
    n_pool, page = cache_sb_k.shape[1], cache_sb_k.shape[2]
    ck = cache_sb_k.reshape(depth, n_pool, page, sbw)
    cv = cache_sb_v.reshape(depth, n_pool, page, sbw)
    cmk = cache_mem_k.reshape(depth, S, M, xaw)
    cmv = cache_mem_v.reshape(depth, S, M, xaw)
    mem_flat = mem_prompt.reshape(B * M, D)
    head_of_lane = jnp.arange(sbw) // SB_HD

    yp, ys = x_prompt, x_sample.reshape(1, S, D)
    outs = [[] for _ in range(14)]
    for l in range(depth):
        lw = _prep_weights(l, p)

        mk = _norm_proj(mem_flat, lw["g_mem"], lw["w_xk"], lw["g_xk"], True, F32).reshape(B, M, xaw)
        mv = _norm_proj(mem_flat, lw["g_mem"], lw["w_xv"], lw["g_xk"], False, F32).reshape(B, M, xaw)

        q, kf, kb, vf, vb, qm, km, vm, og, gcol, grow, u = _in_proj(yp, lw)
        h_sb = _sb_prompt(q, kb, vb, lw["sb_bias"])
        h_ml, Cp, n_p, m_p = _mlstm_prompt(qm, km, vm, og, gcol, grow, lw["g_ml_h"])
        h_cv = _conv_prompt(u, lw["cv_w"], lw["cv_b"], lw["cv_ln_g"], lw["cv_ln_b"])
        xa_p = lambda qq: _xa_prompt(qq.reshape(B, L, xaw), mk, mv).reshape(B * L, xaw)
        yp = _tail(yp.reshape(B * L, D), h_sb.reshape(B * L, -1), h_ml.reshape(B * L, -1),
                   h_cv.reshape(B * L, -1), lw, xa_p).reshape(B, L, D)
        hist_p = u[:, L - (CV_K - 1):, :]
        if L < CV_K - 1:
            hist_p = jnp.concatenate([jnp.zeros((B, CV_K - 1 - L, u.shape[2]), F32), u], axis=1)

        q_s, kf_s, _, vf_s, _, qm_s, km_s, vm_s, og_s, gcol_s, _, u_s = _in_proj(ys, lw)
        qbd = jnp.where(jnp.arange(nh_sb)[None, :, None] == head_of_lane[None, None, :],
                        q_s.reshape(S, 1, sbw), jnp.zeros((), BF16))
        h_sb_s = _sb_sample(l, page_table, qbd, lw["sb_bias"].reshape(nh_sb, 1), ck, cv)
        per_seq = lambda a: a.reshape(S, 1, a.shape[-1])
        h_ml_s, Cs, n_s, m_s = _mlstm_sample(
            per_seq(qm_s), per_seq(km_s), per_seq(vm_s), per_seq(og_s), per_seq(gcol_s), lw["g_ml_h"],
            state_ml_C[l], state_ml_n[l].reshape(S, nh_ml, 1, ML_HD), state_ml_m[l].reshape(S, nh_ml, 1, 1))
        xx = jnp.concatenate([state_conv[l], u_s.reshape(S, 1, -1)], axis=1)
        h_cv_s = _conv_sample(xx, lw["cv_w"], lw["cv_b"], lw["cv_ln_g"], lw["cv_ln_b"])
        xa_s = lambda qq: _xa_sample(l, qq.reshape(S, 1, xaw), cmk, cmv).reshape(S, xaw)
        ys = _tail(ys.reshape(S, D), h_sb_s.reshape(S, -1), h_ml_s.reshape(S, -1), h_cv_s, lw,
                   xa_s).reshape(1, S, D)

        vals = (kf.reshape(B, L, nh_sb, SB_HD), vf.reshape(B, L, nh_sb, SB_HD),
                kf_s.reshape(S, 1, nh_sb, SB_HD), vf_s.reshape(S, 1, nh_sb, SB_HD),
                Cp, n_p.reshape(B, nh_ml, ML_HD), m_p.reshape(B, nh_ml),
                Cs, n_s.reshape(S, nh_ml, ML_HD), m_s.reshape(S, nh_ml),
                hist_p, xx[:, 1:, :],
                mk.reshape(B, M, -1, XA_HD), mv.reshape(B, M, -1, XA_HD))
        for acc, v in zip(outs, vals):
            acc.append(v)
    return (yp, ys.reshape(S, 1, D)) + tuple(jnp.stack(a) for a in outs)
```

```python
import functools

import jax
import jax.numpy as jnp
from jax import lax
from jax.experimental import pallas as pl
from jax.experimental.pallas import tpu as pltpu

F32 = jnp.float32
BF16 = jnp.bfloat16
EPS = 1e-6
NEG_INF = float("-inf")

SB_HD = 64
ML_HD = 128
ML_CHUNK = 128
XA_HD = 128
CV_K = 31
PK_TOPK = 16
PK_NKEYS = 128
LANES = 128
VMEM_LIMIT = 56 * 1024 * 1024


def _params(*sem):
    return pltpu.CompilerParams(dimension_semantics=sem, vmem_limit_bytes=VMEM_LIMIT)


def _dot(a, b):
    return jnp.dot(a, b, preferred_element_type=F32)


def _dot_nt(a, b):
    return lax.dot_general(a, b, (((1,), (1,)), ((), ())), preferred_element_type=F32)


def _split2(x):
    hi = x.astype(BF16)
    lo = (x - hi.astype(F32)).astype(BF16)
    return hi, lo


def _split3(x):
    hi = x.astype(BF16)
    r = x - hi.astype(F32)
    mid = r.astype(BF16)
    lo = (r - mid.astype(F32)).astype(BF16)
    return hi, mid, lo


def _rms(x, g):
    ms = jnp.mean(x * x, axis=-1, keepdims=True)
    return x * lax.rsqrt(ms + EPS) * g


def _log_sigmoid(x):
    return jnp.minimum(x, 0.0) - jnp.log1p(jnp.exp(-jnp.abs(x)))


def _sigmoid(x):
    return 1.0 / (1.0 + jnp.exp(-x))


def _full(shape):
    n = len(shape)
    return pl.BlockSpec(shape, lambda *_: (0,) * n)


def _in_proj_kernel(x_ref, g_ref, wsb_ref, bd_ref, gq_ref, gk_ref, wml_ref,
                    wifh_ref, wifl_ref, bif_ref, wifth_ref, wiftl_ref, bift_ref,
                    wga_ref, wgb_ref,
                    q_ref, kf_ref, kb_ref, vf_ref, vb_ref, qm_ref, km_ref, vm_ref,
                    og_ref, gcol_ref, grow_ref, u_ref):
    x = x_ref[0]
    xn = _rms(x, g_ref[...])
    xh, xl = _split2(xn)

    sbw = q_ref.shape[-1]
    z = _dot(xh, wsb_ref[...])
    bd = bd_ref[...]

    def headnorm(zz, gain):
        sh, sl = _split2(zz * zz)
        ms = (_dot(sh, bd) + _dot(sl, bd)) * (1.0 / SB_HD)
        return zz * lax.rsqrt(ms + EPS) * gain

    qn = headnorm(z[:, :sbw], gq_ref[...])
    kn = headnorm(z[:, sbw:2 * sbw], gk_ref[...])
    vv = z[:, 2 * sbw:]
    q_ref[0] = (qn * (SB_HD ** -0.5)).astype(BF16)
    kf_ref[0] = kn
    kb_ref[0] = kn.astype(BF16)
    vf_ref[0] = vv
    vb_ref[0] = vv.astype(BF16)

    mlw = qm_ref.shape[-1]
    zm = _dot(xh, wml_ref[...])
    qm_ref[0] = zm[:, :mlw].astype(BF16)
    km_ref[0] = (zm[:, mlw:2 * mlw] * (ML_HD ** -0.5)).astype(BF16)
    vm_ref[0] = zm[:, 2 * mlw:3 * mlw].astype(BF16)
    og_ref[0] = _sigmoid(zm[:, 3 * mlw:])

    nh = grow_ref.shape[1] // 2
    wh, wl = wifh_ref[...], wifl_ref[...]
    zc = _dot(xh, wh) + _dot(xh, wl) + _dot(xl, wh) + bif_ref[...]
    lane = lax.broadcasted_iota(jnp.int32, zc.shape, 1)
    gcol_ref[0] = jnp.where(lane >= nh, _log_sigmoid(zc), zc)
    wth, wtl = wifth_ref[...], wiftl_ref[...]
    zr = _dot_nt(wth, xh) + _dot_nt(wtl, xh) + _dot_nt(wth, xl) + bift_ref[...]
    row = lax.broadcasted_iota(jnp.int32, zr.shape, 0)
    grow_ref[0] = jnp.where(row >= nh, _log_sigmoid(zr), zr)

    u_ref[0] = _dot(xh, wga_ref[...]) * _sigmoid(_dot(xh, wgb_ref[...]))


def _in_proj(x, lw):
    B, L, D = x.shape
    tm = min(L, 256)
    sbw = lw["w_sb"].shape[1] // 3
    mlw = lw["w_ml"].shape[1] // 4
    cvw = lw["w_ga"].shape[1]
    nh2 = lw["b_ift"].shape[0]
    tok = lambda c: pl.BlockSpec((1, tm, c), lambda b, i: (b, i, 0))
    weights = [lw["g_mix"], lw["w_sb"], lw["bd64"], lw["g_sb_q"], lw["g_sb_k"], lw["w_ml"],
               lw["w_if_h"], lw["w_if_l"], lw["b_if"], lw["w_ift_h"], lw["w_ift_l"], lw["b_ift"],
               lw["w_ga"], lw["w_gb"]]
    out_shape = [
        jax.ShapeDtypeStruct((B, L, sbw), BF16),
        jax.ShapeDtypeStruct((B, L, sbw), F32),
        jax.ShapeDtypeStruct((B, L, sbw), BF16),
        jax.ShapeDtypeStruct((B, L, sbw), F32),
        jax.ShapeDtypeStruct((B, L, sbw), BF16),
        jax.ShapeDtypeStruct((B, L, mlw), BF16),
        jax.ShapeDtypeStruct((B, L, mlw), BF16),
        jax.ShapeDtypeStruct((B, L, mlw), BF16),
        jax.ShapeDtypeStruct((B, L, mlw), F32),
        jax.ShapeDtypeStruct((B, L, LANES), F32),
        jax.ShapeDtypeStruct((B, nh2, L), F32),
        jax.ShapeDtypeStruct((B, L, cvw), F32),
    ]
    out_specs = [tok(sbw)] * 5 + [tok(mlw)] * 4 + [tok(LANES),
                 pl.BlockSpec((1, nh2, tm), lambda b, i: (b, 0, i)), tok(cvw)]
    return pl.pallas_call(
        _in_proj_kernel,
        grid=(B, L // tm),
        in_specs=[tok(D)] + [_full(w.shape) for w in weights],
        out_specs=out_specs,
        out_shape=out_shape,
        compiler_params=_params("parallel", "parallel"),
        name="in_proj",
    )(x, *weights)


def _split_trunc(x):
    hi = lax.bitcast_convert_type(lax.bitcast_convert_type(x, jnp.uint32) & jnp.uint32(0xFFFF0000), F32)
    return hi.astype(BF16), (x - hi).astype(BF16)


def _sb_logs(z, mask):
    l = jnp.log(1.0 + jnp.exp(-jnp.abs(z)))
    lb = jnp.minimum(z, 0.0) - l
    l1 = lb - z
    if mask is not None:
        l1 = jnp.where(mask, l1, 0.0)
    return lb, l1


def _sb_block(q, k, v, tri, c, bias, mask):
    lb, l1 = _sb_logs(_dot_nt(q, k) + bias, mask)
    hi, lo = _split_trunc(l1)
    within = _dot(hi, tri) + _dot(lo, tri)
    w = jnp.exp(lb + within + c)
    if mask is not None:
        w = jnp.where(mask, w, 0.0)
    pv = _dot(w.astype(BF16), v)
    c_new = c + within[:, 0:1] + l1[:, 0:1]
    return pv, c_new


SB_GROUP = 8


def _sb_prompt_kernel(bias_ref, q_ref, k_ref, v_ref, o_ref, *, tq):
    hg = pl.program_id(1)
    i = pl.program_id(2)
    pair = 2 * SB_HD
    lane = lax.broadcasted_iota(jnp.int32, (tq, pair), 1)
    qs, biases = [], []
    for h in range(SB_GROUP):
        qp = q_ref[0, :, (h // 2) * pair:(h // 2 + 1) * pair]
        keep = (lane < SB_HD) if h % 2 == 0 else (lane >= SB_HD)
        qs.append(jnp.where(keep, qp, jnp.zeros_like(qp)))
        biases.append(bias_ref[SB_GROUP * hg + h])
    row = lax.broadcasted_iota(jnp.int32, (tq, tq), 0)
    col = lax.broadcasted_iota(jnp.int32, (tq, tq), 1)
    tri = (row > col).astype(BF16)
    causal = col < row

    def step(j, carry, mask):
        start = pl.multiple_of(j * tq, tq)
        heads = range(SB_GROUP)
        cols = [slice((h // 2) * pair, (h // 2 + 1) * pair) for h in heads]
        z = [_dot_nt(qs[h], k_ref[0, pl.ds(start, tq), cols[h]]) + biases[h] for h in heads]
        logs = [_sb_logs(z[h], mask) for h in heads]
        parts = [_split_trunc(logs[h][1]) for h in heads]
        within = [_dot(parts[h][0], tri) + _dot(parts[h][1], tri) for h in heads]
        w = []
        for h in heads:
            wh = jnp.exp(logs[h][0] + within[h] + carry[h][1])
            w.append((jnp.where(mask, wh, 0.0) if mask is not None else wh).astype(BF16))
        out = []
        for h in heads:
            pv = _dot(w[h], v_ref[0, pl.ds(start, tq), cols[h]])
            c = carry[h][1] + within[h][:, 0:1] + logs[h][1][:, 0:1]
            out.append((carry[h][0] + pv, c))
        return tuple(out)

    init = tuple((jnp.zeros((tq, pair), F32), jnp.zeros((tq, 1), F32)) for _ in range(SB_GROUP))
    carry = step(i, init, causal)
    carry = lax.fori_loop(0, i, lambda t, cr: step(i - 1 - t, cr, None), carry)
    for p in range(SB_GROUP // 2):
        o_ref[0, :, p * pair:(p + 1) * pair] = jnp.where(
            lane < SB_HD, carry[2 * p][0], carry[2 * p + 1][0]).astype(o_ref.dtype)


def _sb_prompt(q, k, v, bias):
    B, L, W = q.shape
    tq = min(L, 256)
    gw = SB_GROUP * SB_HD
    return pl.pallas_call(
        functools.partial(_sb_prompt_kernel, tq=tq),
        grid_spec=pltpu.PrefetchScalarGridSpec(
            num_scalar_prefetch=1,
            grid=(B, W // gw, L // tq),
            in_specs=[pl.BlockSpec((1, tq, gw), lambda b, h, i, *_: (b, i, h)),
                      pl.BlockSpec((1, L, gw), lambda b, h, i, *_: (b, 0, h)),
                      pl.BlockSpec((1, L, gw), lambda b, h, i, *_: (b, 0, h))],
            out_specs=pl.BlockSpec((1, tq, gw), lambda b, h, i, *_: (b, i, h)),
        ),
        out_shape=jax.ShapeDtypeStruct((B, L, W), BF16),
        compiler_params=_params("parallel", "parallel", "parallel"),
        name="sb_prompt",
    )(bias, q, k, v)


def _sb_sample_kernel(pt_ref, qbd_ref, bias_ref, *refs, n_pages, page):
    k_refs = refs[:n_pages]
    v_refs = refs[n_pages:2 * n_pages]
    o_ref = refs[2 * n_pages]
    qbd = qbd_ref[0]
    nh, W = qbd.shape
    bias = bias_ref[...]
    row = lax.broadcasted_iota(jnp.int32, (page, page), 0)
    col = lax.broadcasted_iota(jnp.int32, (page, page), 1)
    tri = (row > col).astype(BF16)
    acc = jnp.zeros((nh, W), F32)
    c = jnp.zeros((nh, 1), F32)
    for p in reversed(range(n_pages)):
        k = k_refs[p][...].astype(BF16)
        v = v_refs[p][...].astype(BF16)
        pv, c = _sb_block(qbd, k, v, tri, c, bias, None)
        acc = acc + pv
    hrow = lax.broadcasted_iota(jnp.int32, (nh, W), 0)
    hlane = lax.broadcasted_iota(jnp.int32, (nh, W), 1) // SB_HD
    o_ref[0] = jnp.sum(jnp.where(hrow == hlane, acc, 0.0), axis=0, keepdims=True).astype(o_ref.dtype)


def _sb_sample(layer, page_table, qbd, bias_col, cache_k, cache_v):
    S, n_pages = page_table.shape
    _, _, page, W = cache_k.shape
    nh = qbd.shape[1]

    def page_spec(p):
        return pl.BlockSpec((None, None, page, W), lambda b, pt: (layer, pt[b, p], 0, 0))

    return pl.pallas_call(
        functools.partial(_sb_sample_kernel, n_pages=n_pages, page=page),
        grid_spec=pltpu.PrefetchScalarGridSpec(
            num_scalar_prefetch=1,
            grid=(S,),
            in_specs=[pl.BlockSpec((1, nh, W), lambda b, pt: (b, 0, 0)),
                      pl.BlockSpec((nh, 1), lambda b, pt: (0, 0))]
                     + [page_spec(p) for p in range(n_pages)] * 2,
            out_specs=pl.BlockSpec((1, 1, W), lambda b, pt: (b, 0, 0)),
        ),
        out_shape=jax.ShapeDtypeStruct((S, 1, W), BF16),
        compiler_params=_params("parallel"),
        name="sb_sample",
    )(page_table, qbd, bias_col, *([cache_k] * n_pages), *([cache_v] * n_pages))


def _mlstm_chunk_math(q, k, v, ig_col, ig_row, bcum_col, bcum_row, C, n, m):
    L = q.shape[0]
    row = lax.broadcasted_iota(jnp.int32, (L, L), 0)
    col = lax.broadcasted_iota(jnp.int32, (L, L), 1)
    log_d = jnp.where(col <= row, bcum_col - bcum_row + ig_row, NEG_INF)
    log_c = bcum_col + m
    m_t = jnp.maximum(log_c, jnp.max(log_d, axis=-1, keepdims=True))
    w_d = jnp.exp(log_d - m_t)
    w_c = jnp.exp(log_c - m_t)
    s = _dot_nt(q, k) * w_d
    num = _dot(s.astype(BF16), v) + w_c * _dot(q, C.astype(BF16))
    qf = q.astype(F32)
    den = jnp.sum(s, axis=-1, keepdims=True) + w_c * jnp.sum(qf * n, axis=-1, keepdims=True)
    h = num / jnp.maximum(jnp.abs(den), jnp.exp(-m_t))
    b_end = bcum_col[L - 1:L, :]
    log_in = b_end - bcum_col + ig_col
    m_new = jnp.maximum(b_end + m, jnp.max(log_in, axis=0, keepdims=True))
    w_in = jnp.exp(log_in - m_new)
    decay = jnp.exp(b_end + m - m_new)
    kw = k.astype(F32) * w_in
    C_new = decay * C + _dot(kw.T.astype(BF16), v)
    n_new = decay * n + jnp.sum(kw, axis=0, keepdims=True)
    return h, C_new, n_new, m_new


def _mlstm_prompt_kernel(q_ref, k_ref, v_ref, og_ref, gcol_ref, grow_ref, gh_ref,
                         h_ref, c_out, n_out, m_out, c_scr, n_scr, m_scr):
    t = pl.program_id(1)
    nh = c_scr.shape[0]
    L = q_ref.shape[1]

    @pl.when(t == 0)
    def _():
        c_scr[...] = jnp.zeros_like(c_scr)
        n_scr[...] = jnp.zeros_like(n_scr)
        m_scr[...] = jnp.zeros_like(m_scr)

    row = lax.broadcasted_iota(jnp.int32, (L, L), 0)
    col = lax.broadcasted_iota(jnp.int32, (L, L), 1)
    lower = (col <= row).astype(BF16)
    upper = (row <= col).astype(BF16)
    gcol = gcol_ref[0]
    grow = grow_ref[0]
    cum_col = sum(_dot(lower, part) for part in _split3(gcol))
    cum_row = sum(_dot(part, upper) for part in _split3(grow))
    for h in range(nh):
        sl = slice(h * ML_HD, (h + 1) * ML_HD)
        q, k, v = q_ref[0, :, sl], k_ref[0, :, sl], v_ref[0, :, sl]
        hh, C, n, m = _mlstm_chunk_math(
            q, k, v, gcol[:, h:h + 1], grow[h:h + 1, :],
            cum_col[:, nh + h:nh + h + 1], cum_row[nh + h:nh + h + 1, :],
            c_scr[h], n_scr[h], m_scr[h])
        c_scr[h] = C
        n_scr[h] = n
        m_scr[h] = m
        hn = _rms(hh, gh_ref[h:h + 1, :])
        h_ref[0, :, sl] = (og_ref[0, :, sl] * hn).astype(h_ref.dtype)

    @pl.when(t == pl.num_programs(1) - 1)
    def _():
        c_out[0] = c_scr[...]
        n_out[0] = n_scr[...]
        m_out[0] = m_scr[...]


def _mlstm_prompt(qm, km, vm, og, gcol, grow, g_h):
    B, L, W = qm.shape
    nh = W // ML_HD
    ch = min(L, ML_CHUNK)
    tok = lambda c: pl.BlockSpec((1, ch, c), lambda b, t: (b, t, 0))
    return pl.pallas_call(
        _mlstm_prompt_kernel,
        grid=(B, L // ch),
        in_specs=[tok(W), tok(W), tok(W), tok(W), tok(gcol.shape[-1]),
                  pl.BlockSpec((1, 2 * nh, ch), lambda b, t: (b, 0, t)),
                  _full(g_h.shape)],
        out_specs=[tok(W),
                   pl.BlockSpec((1, nh, ML_HD, ML_HD), lambda b, t: (b, 0, 0, 0)),
                   pl.BlockSpec((1, nh, 1, ML_HD), lambda b, t: (b, 0, 0, 0)),
                   pl.BlockSpec((1, nh, 1, 1), lambda b, t: (b, 0, 0, 0))],
        out_shape=[jax.ShapeDtypeStruct((B, L, W), BF16),
                   jax.ShapeDtypeStruct((B, nh, ML_HD, ML_HD), F32),
                   jax.ShapeDtypeStruct((B, nh, 1, ML_HD), F32),
                   jax.ShapeDtypeStruct((B, nh, 1, 1), F32)],
        scratch_shapes=[pltpu.VMEM((nh, ML_HD, ML_HD), F32),
                        pltpu.VMEM((nh, 1, ML_HD), F32),
                        pltpu.VMEM((nh, 1, 1), F32)],
        compiler_params=_params("parallel", "arbitrary"),
        name="mlstm_prompt",
    )(qm, km, vm, og, gcol, grow, g_h)


def _mlstm_sample_kernel(q_ref, k_ref, v_ref, og_ref, g_ref, gh_ref, c_ref, n_ref, m_ref,
                         h_ref, c_out, n_out, m_out):
    nh = c_ref.shape[1]
    d = c_ref.shape[-1]
    eye = (lax.broadcasted_iota(jnp.int32, (d, d), 0) == lax.broadcasted_iota(jnp.int32, (d, d), 1))

    def to_col(r):
        return jnp.sum(jnp.where(eye, r, 0.0), axis=1, keepdims=True)

    g = g_ref[0]
    for h in range(nh):
        sl = slice(h * d, (h + 1) * d)
        q = q_ref[0, :, sl].astype(F32)
        k = k_ref[0, :, sl].astype(F32)
        v = v_ref[0, :, sl].astype(F32)
        C = c_ref[0, h]
        n = n_ref[0, h]
        m = m_ref[0, h]
        ig = g[:, h:h + 1]
        lf = g[:, nh + h:nh + h + 1]
        log_c = lf + m
        m_t = jnp.maximum(log_c, ig)
        w_d = jnp.exp(ig - m_t)
        w_c = jnp.exp(log_c - m_t)
        s = jnp.sum(q * k, axis=-1, keepdims=True) * w_d
        cb = C.astype(BF16).astype(F32)
        qC = jnp.sum(to_col(q) * cb, axis=0, keepdims=True)
        num = s * v + w_c * qC
        den = s + w_c * jnp.sum(q * n, axis=-1, keepdims=True)
        hh = num / jnp.maximum(jnp.abs(den), jnp.exp(-m_t))
        c_out[0, h] = w_c * C + w_d * (to_col(k) * v)
        n_out[0, h] = w_c * n + w_d * k
        m_out[0, h] = m_t
        hn = _rms(hh, gh_ref[h:h + 1, :])
        h_ref[0, :, sl] = (og_ref[0, :, sl] * hn).astype(h_ref.dtype)


def _mlstm_sample(qm, km, vm, og, gcol, g_h, C, n, m):
    S, _, W = qm.shape
    nh = C.shape[1]
    d = C.shape[-1]
    seq = lambda *tail: pl.BlockSpec((1,) + tail, lambda b: (b,) + (0,) * len(tail))
    return pl.pallas_call(
        _mlstm_sample_kernel,
        grid=(S,),
        in_specs=[seq(1, W), seq(1, W), seq(1, W), seq(1, W), seq(1, gcol.shape[-1]), _full(g_h.shape),
                  seq(nh, d, d), seq(nh, 1, d), seq(nh, 1, 1)],
        out_specs=[seq(1, W), seq(nh, d, d), seq(nh, 1, d), seq(nh, 1, 1)],
        out_shape=[jax.ShapeDtypeStruct((S, 1, W), BF16),
                   jax.ShapeDtypeStruct(C.shape, F32),
                   jax.ShapeDtypeStruct(n.shape, F32),
                   jax.ShapeDtypeStruct(m.shape, F32)],
        compiler_params=_params("parallel"),
        name="mlstm_sample",
    )(qm, km, vm, og, gcol, g_h, C, n, m)


def _ln_swish(c, g, b):
    cc = c - jnp.mean(c, axis=-1, keepdims=True)
    y = cc * lax.rsqrt(jnp.mean(cc * cc, axis=-1, keepdims=True) + EPS) * g + b
    return y * _sigmoid(y)


CV_HALO = 32


def _conv_prompt_kernel(u_ref, halo_ref, w_ref, b_ref, g_ref, beta_ref, o_ref, win_ref):
    i = pl.program_id(1)
    tl = u_ref.shape[1]
    halo = halo_ref[0]
    win_ref[0:CV_HALO, :] = jnp.where(i == 0, jnp.zeros_like(halo), halo)
    win_ref[CV_HALO:, :] = u_ref[0]
    off = CV_HALO - (CV_K - 1)
    acc = jnp.zeros((tl, u_ref.shape[2]), F32) + b_ref[...]
    for j in range(CV_K):
        acc = acc + w_ref[j:j + 1, :] * win_ref[off + j:off + j + tl, :]
    o_ref[0] = _ln_swish(acc, g_ref[...], beta_ref[...]).astype(o_ref.dtype)


def _conv_prompt(u, w, b, g, beta):
    B, L, C = u.shape
    tl = min(L, 256)
    r = tl // CV_HALO
    return pl.pallas_call(
        _conv_prompt_kernel,
        grid=(B, L // tl),
        in_specs=[pl.BlockSpec((1, tl, C), lambda b_, i: (b_, i, 0)),
                  pl.BlockSpec((1, CV_HALO, C), lambda b_, i: (b_, jnp.maximum(i * r - 1, 0), 0)),
                  _full(w.shape), _full(b.shape), _full(g.shape), _full(beta.shape)],
        out_specs=pl.BlockSpec((1, tl, C), lambda b_, i: (b_, i, 0)),
        out_shape=jax.ShapeDtypeStruct((B, L, C), BF16),
        scratch_shapes=[pltpu.VMEM((tl + CV_HALO, C), F32)],
        compiler_params=_params("parallel", "parallel"),
        name="conv_prompt",
    )(u, u, w, b, g, beta)


def _conv_sample_kernel(xx_ref, w_ref, b_ref, g_ref, beta_ref, o_ref):
    xx = xx_ref[...]
    c = jnp.sum(xx * w_ref[...][None], axis=1) + b_ref[...]
    o_ref[...] = _ln_swish(c, g_ref[...], beta_ref[...]).astype(o_ref.dtype)


def _conv_sample(xx, w, b, g, beta):
    S, K, C = xx.shape
    nb = min(S, 8)
    return pl.pallas_call(
        _conv_sample_kernel,
        grid=(S // nb,),
        in_specs=[pl.BlockSpec((nb, K, C), lambda s: (s, 0, 0)),
                  _full(w.shape), _full(b.shape), _full(g.shape), _full(beta.shape)],
        out_specs=pl.BlockSpec((nb, C), lambda s: (s, 0)),
        out_shape=jax.ShapeDtypeStruct((S, C), BF16),
        compiler_params=_params("parallel"),
        name="conv_sample",
    )(xx, w, b, g, beta)


def _merge_kernel(x_ref, g_ref, wg_ref, hs_ref, hm_ref, hc_ref, ws_ref, wm_ref, wc_ref, wo_ref, o_ref):
    x = x_ref[...]
    D = x.shape[-1]
    xn = _rms(x, g_ref[...]).astype(BF16)
    merged = None
    for b, (h_ref, w_ref) in enumerate(((hs_ref, ws_ref), (hm_ref, wm_ref), (hc_ref, wc_ref))):
        gate = _sigmoid(_dot(xn, wg_ref[:, b * D:(b + 1) * D]))
        term = gate * _dot(h_ref[...], w_ref[...])
        merged = term if merged is None else merged + term
    o_ref[...] = x + _dot(merged.astype(BF16), wo_ref[...])


def _merge(x, h_sb, h_ml, h_cv, lw):
    T, D = x.shape
    tm = min(T, 256)
    W = h_sb.shape[1]
    tok = lambda c: pl.BlockSpec((tm, c), lambda i: (i, 0))
    weights = [lw["w_sb_o"], lw["w_ml_o"], lw["w_cv_o"], lw["w_out"]]
    return pl.pallas_call(
        _merge_kernel,
        grid=(T // tm,),
        in_specs=[tok(D), _full(lw["g_mix"].shape), _full(lw["w_gates"].shape), tok(W), tok(W), tok(W)]
                 + [_full(w.shape) for w in weights],
        out_specs=tok(D),
        out_shape=jax.ShapeDtypeStruct((T, D), F32),
        compiler_params=_params("parallel"),
        name="merge",
    )(x, lw["g_mix"], lw["w_gates"], h_sb, h_ml, h_cv, *weights)


def _norm_proj_kernel(x_ref, g_ref, w_ref, gh_ref, o_ref, *, head_norm):
    xn = _rms(x_ref[...], g_ref[...]).astype(BF16)
    z = _dot(xn, w_ref[...])
    if head_norm:
        for h in range(z.shape[1] // XA_HD):
            sl = slice(h * XA_HD, (h + 1) * XA_HD)
            o_ref[:, sl] = _rms(z[:, sl], gh_ref[...]).astype(o_ref.dtype)
    else:
        o_ref[...] = z.astype(o_ref.dtype)


def _norm_proj(x, g, w, gh, head_norm, out_dtype):
    T, D = x.shape
    N = w.shape[1]
    tm = min(T, 256)
    return pl.pallas_call(
        functools.partial(_norm_proj_kernel, head_norm=head_norm),
        grid=(T // tm,),
        in_specs=[pl.BlockSpec((tm, D), lambda i: (i, 0)), _full(g.shape), _full(w.shape), _full(gh.shape)],
        out_specs=pl.BlockSpec((tm, N), lambda i: (i, 0)),
        out_shape=jax.ShapeDtypeStruct((T, N), out_dtype),
        compiler_params=_params("parallel"),
        name="norm_proj",
    )(x, g, w, gh)


def _proj_res_kernel(x_ref, h_ref, w_ref, o_ref):
    o_ref[...] = x_ref[...] + _dot(h_ref[...], w_ref[...])


def _proj_res(x, h, w):
    T, D = x.shape
    tm = min(T, 256)
    return pl.pallas_call(
        _proj_res_kernel,
        grid=(T // tm,),
        in_specs=[pl.BlockSpec((tm, D), lambda i: (i, 0)),
                  pl.BlockSpec((tm, h.shape[1]), lambda i: (i, 0)), _full(w.shape)],
        out_specs=pl.BlockSpec((tm, D), lambda i: (i, 0)),
        out_shape=jax.ShapeDtypeStruct((T, D), F32),
        compiler_params=_params("parallel"),
        name="proj_res",
    )(x, h, w)


def _xa_prompt_kernel(q_ref, k_ref, v_ref, o_ref):
    k = k_ref[0].astype(BF16)
    v = v_ref[0].astype(BF16)
    for h in range(q_ref.shape[-1] // XA_HD):
        sl = slice(h * XA_HD, (h + 1) * XA_HD)
        s = _dot_nt(q_ref[0, :, sl], k[:, sl]) * (XA_HD ** -0.5)
        e = jnp.exp(s - jnp.max(s, axis=-1, keepdims=True))
        p = e / jnp.sum(e, axis=-1, keepdims=True)
        o_ref[0, :, sl] = _dot(p.astype(BF16), v[:, sl]).astype(o_ref.dtype)


def _xa_prompt(q, mk, mv):
    B, L, W = q.shape
    M = mk.shape[1]
    tm = min(L, 512)
    return pl.pallas_call(
        _xa_prompt_kernel,
        grid=(B, L // tm),
        in_specs=[pl.BlockSpec((1, tm, W), lambda b, i: (b, i, 0)),
                  pl.BlockSpec((1, M, W), lambda b, i: (b, 0, 0)),
                  pl.BlockSpec((1, M, W), lambda b, i: (b, 0, 0))],
        out_specs=pl.BlockSpec((1, tm, W), lambda b, i: (b, i, 0)),
        out_shape=jax.ShapeDtypeStruct((B, L, W), BF16),
        compiler_params=_params("parallel", "parallel"),
        name="xa_prompt",
    )(q, mk, mv)


def _xa_sample_kernel(q_ref, k_ref, v_ref, o_ref):
    q = q_ref[0].astype(F32)
    k = k_ref[...].astype(BF16).astype(F32)
    v = v_ref[...].astype(BF16).astype(F32)
    prod = k * q
    for h in range(q.shape[-1] // XA_HD):
        sl = slice(h * XA_HD, (h + 1) * XA_HD)
        s = jnp.sum(prod[:, sl], axis=-1, keepdims=True) * (XA_HD ** -0.5)
        e = jnp.exp(s - jnp.max(s, axis=0, keepdims=True))
        p = e / jnp.sum(e, axis=0, keepdims=True)
        p = p.astype(BF16).astype(F32)
        o_ref[0, :, sl] = jnp.sum(p * v[:, sl], axis=0, keepdims=True).astype(o_ref.dtype)


def _xa_sample(layer, q, cache_k, cache_v):
    S, _, W = q.shape
    M = cache_k.shape[2]
    mem = pl.BlockSpec((None, None, M, W), lambda b: (layer, b, 0, 0))
    return pl.pallas_call(
        _xa_sample_kernel,
        grid=(S,),
        in_specs=[pl.BlockSpec((1, 1, W), lambda b: (b, 0, 0)), mem, mem],
        out_specs=pl.BlockSpec((1, 1, W), lambda b: (b, 0, 0)),
        out_shape=jax.ShapeDtypeStruct((S, 1, W), BF16),
        compiler_params=_params("parallel"),
        name="xa_sample",
    )(q, cache_k, cache_v)


NOT_TAKEN = 99.0


def _take_top(s, n):
    idx = lax.broadcasted_iota(jnp.int32, s.shape, 0)
    big = jnp.int32(s.shape[0])
    slot = lax.broadcasted_iota(jnp.int32, (n, s.shape[1]), 0)
    vals = jnp.zeros((n, s.shape[1]), F32)
    rank = jnp.full(s.shape, NOT_TAKEN, F32)
    for r in range(n):
        mx = jnp.max(s, axis=0, keepdims=True)
        first = jnp.min(jnp.where(s == mx, idx, big), axis=0, keepdims=True)
        hit = idx == first
        s = jnp.where(hit, NEG_INF, s)
        rank = jnp.where(hit, float(r), rank)
        vals = jnp.where(slot == r, mx, vals)
    return vals, rank


def _cand_pieces(v1, v2):
    sub = lax.broadcasted_iota(jnp.int32, (8, v1.shape[1]), 0)
    pieces = [v1[0:1] + v2]
    for i in range(1, 8):
        nj = PK_TOPK // (i + 1)
        pieces.append(jnp.where(sub < nj, v1[i:i + 1] + v2[0:8], NEG_INF))
    pieces.append(v1[8:16] + v2[0:1])
    return jnp.concatenate(pieces, axis=0)


def _peer_topk_kernel(x_ref, g_ref, wq_ref, k1h_ref, k1l_ref, k2h_ref, k2l_ref,
                      xt_ref, na_ref, r2_ref, e1_ref, e2_ref):
    xn = _rms(x_ref[...], g_ref[...])
    xt_ref[...] = xn.T.astype(BF16)
    q = _dot(xn.astype(BF16), wq_ref[...])
    nheads = k1h_ref.shape[0]
    half = k1h_ref.shape[2]
    for h in range(nheads):
        q1h, q1l = _split2(q[:, (2 * h) * half:(2 * h + 1) * half])
        q2h, q2l = _split2(q[:, (2 * h + 1) * half:(2 * h + 2) * half])
        s1 = _dot_nt(k1h_ref[h], q1h) + _dot_nt(k1l_ref[h], q1h) + _dot_nt(k1h_ref[h], q1l)
        s2 = _dot_nt(k2h_ref[h], q2h) + _dot_nt(k2l_ref[h], q2h) + _dot_nt(k2h_ref[h], q2l)
        v1, r1 = _take_top(s1, PK_TOPK)
        v2, r2 = _take_top(s2, PK_TOPK)
        top, rc = _take_top(_cand_pieces(v1, v2), PK_TOPK)
        zsum = jnp.sum(jnp.exp(top - top[0:1]), axis=0, keepdims=True)
        sel = jnp.where(rc < NOT_TAKEN, 1.0, 0.0)
        na = jnp.zeros_like(s1)
        for i in range(PK_TOPK):
            if i == 0:
                ni = jnp.sum(sel[0:PK_TOPK], axis=0, keepdims=True)
            elif i < 8:
                ni = jnp.sum(sel[PK_TOPK + 8 * (i - 1):PK_TOPK + 8 * i], axis=0, keepdims=True)
            else:
                ni = sel[PK_TOPK + 48 + i:PK_TOPK + 49 + i]
            na = jnp.where(r1 == float(i), ni, na)
        na_ref[h] = na
        r2_ref[h] = r2
        e1_ref[h] = jnp.where(r1 < NOT_TAKEN, jnp.exp(s1 - v1[0:1]) / zsum, 0.0)
        e2_ref[h] = jnp.where(r2 < NOT_TAKEN, jnp.exp(s2 - v2[0:1]), 0.0)


def _peer_topk(x, lw):
    T, D = x.shape
    tm = min(T, 256)
    nheads, nkeys, _ = lw["pk_k1_h"].shape
    weights = [lw["g_ffn"], lw["w_pq"], lw["pk_k1_h"], lw["pk_k1_l"], lw["pk_k2_h"], lw["pk_k2_l"]]
    key_arr = lambda dt: jax.ShapeDtypeStruct((nheads, nkeys, T), dt)
    key_spec = pl.BlockSpec((nheads, nkeys, tm), lambda i: (0, 0, i))
    return pl.pallas_call(
        _peer_topk_kernel,
        grid=(T // tm,),
        in_specs=[pl.BlockSpec((tm, D), lambda i: (i, 0))] + [_full(w.shape) for w in weights],
        out_specs=[pl.BlockSpec((D, tm), lambda i: (0, i))] + [key_spec] * 4,
        out_shape=[jax.ShapeDtypeStruct((D, T), BF16)] + [key_arr(F32)] * 4,
        compiler_params=_params("parallel"),
        name="peer_topk",
    )(x, *weights)


def _gelu(x):
    return 0.5 * x * (1.0 + lax.erf(x * (2.0 ** -0.5)))


PEER_ROWS = 32
PEER_AGROUP = 4


def _peer_dense_kernel(x_ref, xt_ref, u_ref, vt_ref, na_ref, r2_ref, e1_ref, e2_ref,
                       o_ref, acc_ref, act_ref, gate_ref, w_ref):
    j = pl.program_id(1)
    last = pl.num_programs(1) - 1
    nheads, nkeys, tm = r2_ref.shape
    na = na_ref.shape[1]
    slot = j % 2

    @pl.when(j == 0)
    def _():
        acc_ref[...] = jnp.zeros_like(acc_ref)
        w_ref[1] = jnp.zeros(w_ref.shape[1:], w_ref.dtype)

    cw = min(tm, LANES)
    mw = min(tm, 2 * LANES)
    sub = 8
    blk = (PEER_ROWS // sub, sub, cw)

    def apply_values(cols):
        acc_ref[:, cols] += _dot(vt_ref[...], w_ref[1 - slot, :, cols])

    def activations(cols):
        act_ref[:, cols] = _dot(u_ref[...], xt_ref[:, cols])

    def gate_chunk(c):
        ln = slice(c * cw, (c + 1) * cw)
        for bq in range(nkeys // PEER_ROWS):
            rb = slice(bq * PEER_ROWS, (bq + 1) * PEER_ROWS)
            for ag in range(0, na, PEER_AGROUP):
                g = [jnp.zeros(blk, F32) for _ in range(PEER_AGROUP)]
                for h in range(nheads):
                    r2 = r2_ref[h, rb, ln].reshape(blk)
                    e2 = e2_ref[h, rb, ln].reshape(blk)
                    for k in range(PEER_AGROUP):
                        n_a = jnp.broadcast_to(na_ref[h, ag + k:ag + k + 1, ln], (sub, cw))[None]
                        e1 = jnp.broadcast_to(e1_ref[h, ag + k:ag + k + 1, ln], (sub, cw))[None]
                        g[k] = g[k] + jnp.where(r2 < n_a, e1 * e2, 0.0)
                for k in range(PEER_AGROUP):
                    r0 = (ag + k) * nkeys + bq * PEER_ROWS
                    gate_ref[r0:r0 + PEER_ROWS, ln] = g[k].reshape(PEER_ROWS, cw)

    per = mw // cw
    for m in range(tm // mw):
        cols = slice(m * mw, (m + 1) * mw)
        apply_values(cols)
        for c in range(m * per, m * per + (per + 1) // 2):
            gate_chunk(c)
        activations(cols)
        for c in range(m * per + (per + 1) // 2, (m + 1) * per):
            gate_chunk(c)
    w_ref[slot] = (gate_ref[...] * _gelu(act_ref[...])).astype(BF16)

    @pl.when(j == last)
    def _():
        o_ref[...] = x_ref[...] + acc_ref[...].T


def _peer_dense(x, xt, na_, r2, e1, e2, lw):
    T, D = x.shape
    u, vt = lw["pk_u"], lw["pk_vt"]
    E = u.shape[0]
    nheads, nkeys, _ = r2.shape
    tm = min(T, 512)
    na = 8
    te = na * nkeys
    nt = E // te
    cur = lambda j: jnp.minimum(j, nt - 1)
    key_spec = pl.BlockSpec((nheads, nkeys, tm), lambda i, j: (0, 0, i))
    a_spec = pl.BlockSpec((nheads, na, tm), lambda i, j: (0, cur(j), i))
    return pl.pallas_call(
        _peer_dense_kernel,
        grid=(T // tm, nt + 1),
        in_specs=[pl.BlockSpec((tm, D), lambda i, j: (i, 0)),
                  pl.BlockSpec((D, tm), lambda i, j: (0, i)),
                  pl.BlockSpec((te, D), lambda i, j: (cur(j), 0)),
                  pl.BlockSpec((D, te), lambda i, j: (0, jnp.maximum(j - 1, 0))),
                  a_spec, key_spec, a_spec, key_spec],
        out_specs=pl.BlockSpec((tm, D), lambda i, j: (i, 0)),
        out_shape=jax.ShapeDtypeStruct((T, D), F32),
        scratch_shapes=[pltpu.VMEM((D, tm), F32), pltpu.VMEM((te, tm), F32), pltpu.VMEM((te, tm), F32),
                        pltpu.VMEM((2, te, tm), BF16)],
        compiler_params=_params("parallel", "arbitrary"),
        name="peer_dense",
    )(x, xt, u, vt, na_, r2, e1, e2)


def _prep_weights(l, p):
    w_in = p["w_in"][l]
    D = w_in.shape[0]
    sbw = p["w_sb_o"].shape[1]
    mlw = p["w_ml_o"].shape[1]
    cvw = p["w_cv_o"].shape[1]
    nh = p["ml_ibias"].shape[1]
    o = 0
    w_sb = w_in[:, o:o + 3 * sbw]; o += 3 * sbw
    w_ml = w_in[:, o:o + 4 * mlw]; o += 4 * mlw
    w_if = w_in[:, o:o + 2 * nh]; o += 2 * nh
    w_ga = w_in[:, o:o + cvw]; o += cvw
    w_gb = w_in[:, o:o + cvw]; o += cvw
    w_gates = w_in[:, o:]
    w_if_pad = jnp.pad(w_if, ((0, 0), (0, LANES - 2 * nh)))
    w_if_h, w_if_l = _split2(w_if_pad)
    w_ift_h, w_ift_l = _split2(w_if.T)
    b_if = jnp.concatenate([p["ml_ibias"][l], p["ml_fbias"][l]])
    idx = jnp.arange(sbw) // SB_HD
    k1h, k1l = _split2(p["pk_k1"][l])
    k2h, k2l = _split2(p["pk_k2"][l])
    row = lambda a: a.reshape(1, -1)
    return {
        "g_mix": row(p["g_mix"][l]), "w_sb": w_sb.astype(BF16), "w_ml": w_ml.astype(BF16),
        "bd64": (idx[:, None] == idx[None, :]).astype(BF16),
        "g_sb_q": row(jnp.tile(p["g_sb_q"][l], sbw // SB_HD)),
        "g_sb_k": row(jnp.tile(p["g_sb_k"][l], sbw // SB_HD)),
        "w_if_h": w_if_h, "w_if_l": w_if_l, "b_if": row(jnp.pad(b_if, (0, LANES - 2 * nh))),
        "w_ift_h": w_ift_h, "w_ift_l": w_ift_l, "b_ift": b_if.reshape(-1, 1),
        "w_ga": w_ga.astype(BF16), "w_gb": w_gb.astype(BF16), "w_gates": w_gates.astype(BF16),
        "sb_bias": p["sb_bias"][l], "g_ml_h": p["g_ml_h"][l],
        "cv_w": p["cv_w"][l], "cv_b": row(p["cv_b"][l]),
        "cv_ln_g": row(p["cv_ln_g"][l]), "cv_ln_b": row(p["cv_ln_b"][l]),
        "w_sb_o": p["w_sb_o"][l].astype(BF16), "w_ml_o": p["w_ml_o"][l].astype(BF16),
        "w_cv_o": p["w_cv_o"][l].astype(BF16), "w_out": p["w_out"][l].astype(BF16),
        "g_xa": row(p["g_xa"][l]), "g_mem": row(p["g_mem"][l]),
        "w_xq": p["w_xq"][l].astype(BF16), "w_xk": p["w_xk"][l].astype(BF16),
        "w_xv": p["w_xv"][l].astype(BF16), "g_xq": row(p["g_xq"][l]), "g_xk": row(p["g_xk"][l]),
        "w_xo": p["w_xo"][l].astype(BF16),
        "g_ffn": row(p["g_ffn"][l]), "w_pq": p["w_pq"][l].astype(BF16),
        "pk_k1_h": k1h, "pk_k1_l": k1l, "pk_k2_h": k2h, "pk_k2_l": k2l,
        "pk_u": p["pk_u"][l].astype(BF16), "pk_vt": p["pk_v"][l].T.astype(BF16),
    }


def _tail(x, h_sb, h_ml, h_cv, lw, xa_core):
    x = _merge(x, h_sb, h_ml, h_cv, lw)
    q = _norm_proj(x, lw["g_xa"], lw["w_xq"], lw["g_xq"], True, BF16)
    x = _proj_res(x, xa_core(q), lw["w_xo"])
    xt, na, r2, e1, e2 = _peer_topk(x, lw)
    return _peer_dense(x, xt, na, r2, e1, e2, lw)


def kernel(x_prompt, x_sample, cache_sb_k, cache_sb_v, state_ml_C, state_ml_n, state_ml_m, state_conv, cache_mem_k, cache_mem_v, page_table, mem_prompt, g_mix, w_in, g_sb_q, g_sb_k, sb_bias, ml_ibias, ml_fbias, g_ml_h, cv_w, cv_b, cv_ln_g, cv_ln_b, w_sb_o, w_ml_o, w_cv_o, w_out, g_xa, g_mem, w_xq, w_xk, w_xv, g_xq, g_xk, w_xo, g_ffn, w_pq, pk_k1, pk_k2, pk_u, pk_v):
    p = dict(g_mix=g_mix, w_in=w_in, g_sb_q=g_sb_q, g_sb_k=g_sb_k, sb_bias=sb_bias, ml_ibias=ml_ibias,
             ml_fbias=ml_fbias, g_ml_h=g_ml_h, cv_w=cv_w, cv_b=cv_b, cv_ln_g=cv_ln_g, cv_ln_b=cv_ln_b,
             w_sb_o=w_sb_o, w_ml_o=w_ml_o, w_cv_o=w_cv_o, w_out=w_out, g_xa=g_xa, g_mem=g_mem, w_xq=w_xq,
             w_xk=w_xk, w_xv=w_xv, g_xq=g_xq, g_xk=g_xk, w_xo=w_xo, g_ffn=g_ffn, w_pq=w_pq,
             pk_k1=pk_k1, pk_k2=pk_k2, pk_u=pk_u, pk_v=pk_v)
    depth = w_in.shape[0]
    B, L, D = x_prompt.shape
    S = x_sample.shape[0]
    nh_sb = sb_bias.shape[1]
    nh_ml = ml_ibias.shape[1]
    sbw = w_sb_o.shape[1]
    M = mem_prompt.shape[1]
    xaw = w_xo.shape[1]
    n_pool, page = cache_sb_k.shape[1], cache_sb_k.shape[2]
    ck = cache_sb_k.astype(BF16).reshape(depth, n_pool, page, sbw)
    cv = cache_sb_v.astype(BF16).reshape(depth, n_pool, page, sbw)
    cmk = cache_mem_k.reshape(depth, S, M, xaw)
    cmv = cache_mem_v.reshape(depth, S, M, xaw)
    mem_flat = mem_prompt.reshape(B * M, D)
    head_of_lane = jnp.arange(sbw) // SB_HD

    yp, ys = x_prompt, x_sample.reshape(1, S, D)
    outs = [[] for _ in range(14)]
    for l in range(depth):
        lw = _prep_weights(l, p)

        mk = _norm_proj(mem_flat, lw["g_mem"], lw["w_xk"], lw["g_xk"], True, F32).reshape(B, M, xaw)
        mv = _norm_proj(mem_flat, lw["g_mem"], lw["w_xv"], lw["g_xk"], False, F32).reshape(B, M, xaw)

        q, kf, kb, vf, vb, qm, km, vm, og, gcol, grow, u = _in_proj(yp, lw)
        h_sb = _sb_prompt(q, kb, vb, lw["sb_bias"])
        h_ml, Cp, n_p, m_p = _mlstm_prompt(qm, km, vm, og, gcol, grow, lw["g_ml_h"])
        h_cv = _conv_prompt(u, lw["cv_w"], lw["cv_b"], lw["cv_ln_g"], lw["cv_ln_b"])
        xa_p = lambda qq: _xa_prompt(qq.reshape(B, L, xaw), mk, mv).reshape(B * L, xaw)
        yp = _tail(yp.reshape(B * L, D), h_sb.reshape(B * L, -1), h_ml.reshape(B * L, -1),
                   h_cv.reshape(B * L, -1), lw, xa_p).reshape(B, L, D)
        hist_p = u[:, L - (CV_K - 1):, :]
        if L < CV_K - 1:
            hist_p = jnp.concatenate([jnp.zeros((B, CV_K - 1 - L, u.shape[2]), F32), u], axis=1)

        q_s, kf_s, _, vf_s, _, qm_s, km_s, vm_s, og_s, gcol_s, _, u_s = _in_proj(ys, lw)
        qbd = jnp.where(jnp.arange(nh_sb)[None, :, None] == head_of_lane[None, None, :],
                        q_s.reshape(S, 1, sbw), jnp.zeros((), BF16))
        h_sb_s = _sb_sample(l, page_table, qbd, lw["sb_bias"].reshape(nh_sb, 1), ck, cv)
        per_seq = lambda a: a.reshape(S, 1, a.shape[-1])
        h_ml_s, Cs, n_s, m_s = _mlstm_sample(
            per_seq(qm_s), per_seq(km_s), per_seq(vm_s), per_seq(og_s), per_seq(gcol_s), lw["g_ml_h"],
            state_ml_C[l], state_ml_n[l].reshape(S, nh_ml, 1, ML_HD), state_ml_m[l].reshape(S, nh_ml, 1, 1))
        xx = jnp.concatenate([state_conv[l], u_s.reshape(S, 1, -1)], axis=1)
        h_cv_s = _conv_sample(xx, lw["cv_w"], lw["cv_b"], lw["cv_ln_g"], lw["cv_ln_b"])
        xa_s = lambda qq: _xa_sample(l, qq.reshape(S, 1, xaw), cmk, cmv).reshape(S, xaw)
        ys = _tail(ys.reshape(S, D), h_sb_s.reshape(S, -1), h_ml_s.reshape(S, -1), h_cv_s, lw,
                   xa_s).reshape(1, S, D)

        vals = (kf.reshape(B, L, nh_sb, SB_HD), vf.reshape(B, L, nh_sb, SB_HD),
                kf_s.reshape(S, 1, nh_sb, SB_HD), vf_s.reshape(S, 1, nh_sb, SB_HD),
                Cp, n_p.reshape(B, nh_ml, ML_HD), m_p.reshape(B, nh_ml),
                Cs, n_s.reshape(S, nh_ml, ML_HD), m_s.reshape(S, nh_ml),
                hist_p, xx[:, 1:, :],
                mk.reshape(B, M, -1, XA_HD), mv.reshape(B, M, -1, XA_HD))
        for acc, v in zip(outs, vals):
            acc.append(v)
    return (yp, ys.reshape(S, 1, D)) + tuple(jnp.stack(a) for a in outs)
```

```python
import functools

import jax
import jax.numpy as jnp
from jax import lax
from jax.experimental import pallas as pl
from jax.experimental.pallas import tpu as pltpu

F32 = jnp.float32
BF16 = jnp.bfloat16
EPS = 1e-6
NEG_INF = float("-inf")

SB_HD = 64
ML_HD = 128
ML_CHUNK = 128
XA_HD = 128
CV_K = 31
PK_TOPK = 16
PK_NKEYS = 128
LANES = 128
VMEM_LIMIT = 56 * 1024 * 1024


def _params(*sem):
    return pltpu.CompilerParams(dimension_semantics=sem, vmem_limit_bytes=VMEM_LIMIT)


def _dot(a, b):
    return jnp.dot(a, b, preferred_element_type=F32)


def _dot_nt(a, b):
    return lax.dot_general(a, b, (((1,), (1,)), ((), ())), preferred_element_type=F32)


def _split2(x):
    hi = x.astype(BF16)
    lo = (x - hi.astype(F32)).astype(BF16)
    return hi, lo


def _split3(x):
    hi = x.astype(BF16)
    r = x - hi.astype(F32)
    mid = r.astype(BF16)
    lo = (r - mid.astype(F32)).astype(BF16)
    return hi, mid, lo


def _rms(x, g):
    ms = jnp.mean(x * x, axis=-1, keepdims=True)
    return x * lax.rsqrt(ms + EPS) * g


def _log_sigmoid(x):
    return jnp.minimum(x, 0.0) - jnp.log1p(jnp.exp(-jnp.abs(x)))


def _sigmoid(x):
    return 1.0 / (1.0 + jnp.exp(-x))


def _full(shape):
    n = len(shape)
    return pl.BlockSpec(shape, lambda *_: (0,) * n)


def _in_proj_kernel(x_ref, g_ref, wsb_ref, bd_ref, gq_ref, gk_ref, wml_ref,
                    wifh_ref, wifl_ref, bif_ref, wifth_ref, wiftl_ref, bift_ref,
                    wga_ref, wgb_ref,
                    q_ref, kf_ref, kb_ref, vf_ref, vb_ref, qm_ref, km_ref, vm_ref,
                    og_ref, gcol_ref, grow_ref, u_ref):
    x = x_ref[0]
    xn = _rms(x, g_ref[...])
    xh, xl = _split2(xn)

    sbw = q_ref.shape[-1]
    z = _dot(xh, wsb_ref[...])
    bd = bd_ref[...]

    def headnorm(zz, gain):
        sh, sl = _split2(zz * zz)
        ms = (_dot(sh, bd) + _dot(sl, bd)) * (1.0 / SB_HD)
        return zz * lax.rsqrt(ms + EPS) * gain

    qn = headnorm(z[:, :sbw], gq_ref[...])
    kn = headnorm(z[:, sbw:2 * sbw], gk_ref[...])
    vv = z[:, 2 * sbw:]
    q_ref[0] = (qn * (SB_HD ** -0.5)).astype(BF16)
    kf_ref[0] = kn
    kb_ref[0] = kn.astype(BF16)
    vf_ref[0] = vv
    vb_ref[0] = vv.astype(BF16)

    mlw = qm_ref.shape[-1]
    zm = _dot(xh, wml_ref[...])
    qm_ref[0] = zm[:, :mlw].astype(BF16)
    km_ref[0] = (zm[:, mlw:2 * mlw] * (ML_HD ** -0.5)).astype(BF16)
    vm_ref[0] = zm[:, 2 * mlw:3 * mlw].astype(BF16)
    og_ref[0] = _sigmoid(zm[:, 3 * mlw:])

    nh = grow_ref.shape[1] // 2
    wh, wl = wifh_ref[...], wifl_ref[...]
    zc = _dot(xh, wh) + _dot(xh, wl) + _dot(xl, wh) + bif_ref[...]
    lane = lax.broadcasted_iota(jnp.int32, zc.shape, 1)
    gcol_ref[0] = jnp.where(lane >= nh, _log_sigmoid(zc), zc)
    wth, wtl = wifth_ref[...], wiftl_ref[...]
    zr = _dot_nt(wth, xh) + _dot_nt(wtl, xh) + _dot_nt(wth, xl) + bift_ref[...]
    row = lax.broadcasted_iota(jnp.int32, zr.shape, 0)
    grow_ref[0] = jnp.where(row >= nh, _log_sigmoid(zr), zr)

    u_ref[0] = _dot(xh, wga_ref[...]) * _sigmoid(_dot(xh, wgb_ref[...]))


def _in_proj(x, lw):
    B, L, D = x.shape
    tm = min(L, 256)
    sbw = lw["w_sb"].shape[1] // 3
    mlw = lw["w_ml"].shape[1] // 4
    cvw = lw["w_ga"].shape[1]
    nh2 = lw["b_ift"].shape[0]
    tok = lambda c: pl.BlockSpec((1, tm, c), lambda b, i: (b, i, 0))
    weights = [lw["g_mix"], lw["w_sb"], lw["bd64"], lw["g_sb_q"], lw["g_sb_k"], lw["w_ml"],
               lw["w_if_h"], lw["w_if_l"], lw["b_if"], lw["w_ift_h"], lw["w_ift_l"], lw["b_ift"],
               lw["w_ga"], lw["w_gb"]]
    out_shape = [
        jax.ShapeDtypeStruct((B, L, sbw), BF16),
        jax.ShapeDtypeStruct((B, L, sbw), F32),
        jax.ShapeDtypeStruct((B, L, sbw), BF16),
        jax.ShapeDtypeStruct((B, L, sbw), F32),
        jax.ShapeDtypeStruct((B, L, sbw), BF16),
        jax.ShapeDtypeStruct((B, L, mlw), BF16),
        jax.ShapeDtypeStruct((B, L, mlw), BF16),
        jax.ShapeDtypeStruct((B, L, mlw), BF16),
        jax.ShapeDtypeStruct((B, L, mlw), F32),
        jax.ShapeDtypeStruct((B, L, LANES), F32),
        jax.ShapeDtypeStruct((B, nh2, L), F32),
        jax.ShapeDtypeStruct((B, L, cvw), F32),
    ]
    out_specs = [tok(sbw)] * 5 + [tok(mlw)] * 4 + [tok(LANES),
                 pl.BlockSpec((1, nh2, tm), lambda b, i: (b, 0, i)), tok(cvw)]
    return pl.pallas_call(
        _in_proj_kernel,
        grid=(B, L // tm),
        in_specs=[tok(D)] + [_full(w.shape) for w in weights],
        out_specs=out_specs,
        out_shape=out_shape,
        compiler_params=_params("parallel", "parallel"),
        name="in_proj",
    )(x, *weights)


def _split_trunc(x):
    hi = lax.bitcast_convert_type(lax.bitcast_convert_type(x, jnp.uint32) & jnp.uint32(0xFFFF0000), F32)
    return hi.astype(BF16), (x - hi).astype(BF16)


def _sb_logs(z, mask):
    l = jnp.log(1.0 + jnp.exp(-jnp.abs(z)))
    lb = jnp.minimum(z, 0.0) - l
    l1 = lb - z
    if mask is not None:
        l1 = jnp.where(mask, l1, 0.0)
    return lb, l1


SB_GROUP = 8


def _sb_prompt_kernel(bias_ref, q_ref, k_ref, v_ref, o_ref, *, tq):
    hg = pl.program_id(1)
    i = pl.program_id(2)
    pair = 2 * SB_HD
    lane = lax.broadcasted_iota(jnp.int32, (tq, pair), 1)
    qs, biases = [], []
    for h in range(SB_GROUP):
        qp = q_ref[0, :, (h // 2) * pair:(h // 2 + 1) * pair]
        keep = (lane < SB_HD) if h % 2 == 0 else (lane >= SB_HD)
        qs.append(jnp.where(keep, qp, jnp.zeros_like(qp)))
        biases.append(bias_ref[SB_GROUP * hg + h])
    row = lax.broadcasted_iota(jnp.int32, (tq, tq), 0)
    col = lax.broadcasted_iota(jnp.int32, (tq, tq), 1)
    tri = (row > col).astype(BF16)
    causal = col < row

    def step(j, carry, mask):
        start = pl.multiple_of(j * tq, tq)
        heads = range(SB_GROUP)
        cols = [slice((h // 2) * pair, (h // 2 + 1) * pair) for h in heads]
        z = [_dot_nt(qs[h], k_ref[0, pl.ds(start, tq), cols[h]]) + biases[h] for h in heads]
        logs = [_sb_logs(z[h], mask) for h in heads]
        parts = [_split_trunc(logs[h][1]) for h in heads]
        within = [_dot(parts[h][0], tri) + _dot(parts[h][1], tri) for h in heads]
        w = []
        for h in heads:
            wh = jnp.exp(logs[h][0] + within[h] + carry[h][1])
            w.append((jnp.where(mask, wh, 0.0) if mask is not None else wh).astype(BF16))
        out = []
        for h in heads:
            pv = _dot(w[h], v_ref[0, pl.ds(start, tq), cols[h]])
            c = carry[h][1] + within[h][:, 0:1] + logs[h][1][:, 0:1]
            out.append((carry[h][0] + pv, c))
        return tuple(out)

    init = tuple((jnp.zeros((tq, pair), F32), jnp.zeros((tq, 1), F32)) for _ in range(SB_GROUP))
    carry = step(i, init, causal)
    carry = lax.fori_loop(0, i, lambda t, cr: step(i - 1 - t, cr, None), carry)
    for p in range(SB_GROUP // 2):
        o_ref[0, :, p * pair:(p + 1) * pair] = jnp.where(
            lane < SB_HD, carry[2 * p][0], carry[2 * p + 1][0]).astype(o_ref.dtype)


def _sb_prompt(q, k, v, bias):
    B, L, W = q.shape
    tq = min(L, 256)
    gw = SB_GROUP * SB_HD
    return pl.pallas_call(
        functools.partial(_sb_prompt_kernel, tq=tq),
        grid_spec=pltpu.PrefetchScalarGridSpec(
            num_scalar_prefetch=1,
            grid=(B, W // gw, L // tq),
            in_specs=[pl.BlockSpec((1, tq, gw), lambda b, h, i, *_: (b, i, h)),
                      pl.BlockSpec((1, L, gw), lambda b, h, i, *_: (b, 0, h)),
                      pl.BlockSpec((1, L, gw), lambda b, h, i, *_: (b, 0, h))],
            out_specs=pl.BlockSpec((1, tq, gw), lambda b, h, i, *_: (b, i, h)),
        ),
        out_shape=jax.ShapeDtypeStruct((B, L, W), BF16),
        compiler_params=_params("parallel", "parallel", "parallel"),
        name="sb_prompt",
    )(bias, q, k, v)


SB_PAGES_PER_STEP = 8


def _sb_sample_kernel(pt_ref, q_ref, bias_ref, *refs, pps, page):
    k_refs = refs[:pps]
    v_refs = refs[pps:2 * pps]
    o_ref, acc_ref, c_ref = refs[2 * pps:]
    g = pl.program_id(1)
    q = q_ref[0]
    nh = q.shape[0]
    n = page * nh
    bias = bias_ref[...]

    @pl.when(g == 0)
    def _():
        acc_ref[...] = jnp.zeros_like(acc_ref)
        c_ref[...] = jnp.zeros_like(c_ref)

    row = lax.broadcasted_iota(jnp.int32, (nh, n), 0)
    lane = lax.broadcasted_iota(jnp.int32, (nh, n), 1)
    own = (lane % nh) == row
    pages = range(pps)
    z = [_dot_nt(q, k_refs[t][...].astype(BF16)) + bias for t in pages]
    logs = [_sb_logs(z[t], own) for t in pages]
    incl = []
    for t in pages:
        x = logs[t][1]
        d = nh
        while d < n:
            x = x + jnp.where(lane < n - d, pltpu.roll(x, n - d, axis=1), 0.0)
            d *= 2
        incl.append(x)
    c = c_ref[...]
    acc = acc_ref[...]
    for t in pages:
        tail = incl[t] - logs[t][1] + c
        w = jnp.where(own, jnp.exp(logs[t][0] + tail), 0.0)
        acc = acc + _dot(w.astype(BF16), v_refs[t][...].astype(BF16))
        c = c + jnp.sum(jnp.where(lane == row, incl[t], 0.0), axis=1, keepdims=True)
    acc_ref[...] = acc
    c_ref[...] = c

    @pl.when(g == pl.num_programs(1) - 1)
    def _():
        o_ref[0] = acc.astype(o_ref.dtype)


def _sb_sample(layer, page_table, q, bias_col, cache_k, cache_v):
    S, n_pages = page_table.shape
    _, _, rows, hd = cache_k.shape
    nh = q.shape[1]
    pps = min(SB_PAGES_PER_STEP, n_pages)

    def page_spec(t):
        return pl.BlockSpec((None, None, rows, hd),
                            lambda b, g, pt: (layer, pt[b, n_pages - 1 - (g * pps + t)], 0, 0))

    return pl.pallas_call(
        functools.partial(_sb_sample_kernel, pps=pps, page=rows // nh),
        grid_spec=pltpu.PrefetchScalarGridSpec(
            num_scalar_prefetch=1,
            grid=(S, n_pages // pps),
            in_specs=[pl.BlockSpec((1, nh, hd), lambda b, g, pt: (b, 0, 0)),
                      pl.BlockSpec((nh, 1), lambda b, g, pt: (0, 0))]
                     + [page_spec(t) for t in range(pps)] * 2,
            out_specs=pl.BlockSpec((1, nh, hd), lambda b, g, pt: (b, 0, 0)),
            scratch_shapes=[pltpu.VMEM((nh, hd), F32), pltpu.VMEM((nh, 1), F32)],
        ),
        out_shape=jax.ShapeDtypeStruct((S, nh, hd), BF16),
        compiler_params=_params("parallel", "arbitrary"),
        name="sb_sample",
    )(page_table, q, bias_col, *([cache_k] * pps), *([cache_v] * pps))


def _mlstm_chunk_math(q, k, v, ig_col, ig_row, bcum_col, bcum_row, C, n, m):
    L = q.shape[0]
    row = lax.broadcasted_iota(jnp.int32, (L, L), 0)
    col = lax.broadcasted_iota(jnp.int32, (L, L), 1)
    log_d = jnp.where(col <= row, bcum_col - bcum_row + ig_row, NEG_INF)
    log_c = bcum_col + m
    m_t = jnp.maximum(log_c, jnp.max(log_d, axis=-1, keepdims=True))
    w_d = jnp.exp(log_d - m_t)
    w_c = jnp.exp(log_c - m_t)
    s = _dot_nt(q, k) * w_d
    num = _dot(s.astype(BF16), v) + w_c * _dot(q, C.astype(BF16))
    qf = q.astype(F32)
    den = jnp.sum(s, axis=-1, keepdims=True) + w_c * jnp.sum(qf * n, axis=-1, keepdims=True)
    h = num / jnp.maximum(jnp.abs(den), jnp.exp(-m_t))
    b_end = bcum_col[L - 1:L, :]
    log_in = b_end - bcum_col + ig_col
    m_new = jnp.maximum(b_end + m, jnp.max(log_in, axis=0, keepdims=True))
    w_in = jnp.exp(log_in - m_new)
    decay = jnp.exp(b_end + m - m_new)
    kw = k.astype(F32) * w_in
    C_new = decay * C + _dot(kw.T.astype(BF16), v)
    n_new = decay * n + jnp.sum(kw, axis=0, keepdims=True)
    return h, C_new, n_new, m_new


def _mlstm_prompt_kernel(q_ref, k_ref, v_ref, og_ref, gcol_ref, grow_ref, gh_ref,
                         h_ref, c_out, n_out, m_out, c_scr, n_scr, m_scr):
    t = pl.program_id(1)
    nh = c_scr.shape[0]
    L = q_ref.shape[1]

    @pl.when(t == 0)
    def _():
        c_scr[...] = jnp.zeros_like(c_scr)
        n_scr[...] = jnp.zeros_like(n_scr)
        m_scr[...] = jnp.zeros_like(m_scr)

    row = lax.broadcasted_iota(jnp.int32, (L, L), 0)
    col = lax.broadcasted_iota(jnp.int32, (L, L), 1)
    lower = (col <= row).astype(BF16)
    upper = (row <= col).astype(BF16)
    gcol = gcol_ref[0]
    grow = grow_ref[0]
    cum_col = sum(_dot(lower, part) for part in _split3(gcol))
    cum_row = sum(_dot(part, upper) for part in _split3(grow))
    for h in range(nh):
        sl = slice(h * ML_HD, (h + 1) * ML_HD)
        q, k, v = q_ref[0, :, sl], k_ref[0, :, sl], v_ref[0, :, sl]
        hh, C, n, m = _mlstm_chunk_math(
            q, k, v, gcol[:, h:h + 1], grow[h:h + 1, :],
            cum_col[:, nh + h:nh + h + 1], cum_row[nh + h:nh + h + 1, :],
            c_scr[h], n_scr[h], m_scr[h])
        c_scr[h] = C
        n_scr[h] = n
        m_scr[h] = m
        hn = _rms(hh, gh_ref[h:h + 1, :])
        h_ref[0, :, sl] = (og_ref[0, :, sl] * hn).astype(h_ref.dtype)

    @pl.when(t == pl.num_programs(1) - 1)
    def _():
        c_out[0] = c_scr[...]
        n_out[0] = n_scr[...]
        m_out[0] = m_scr[...]


def _mlstm_prompt(qm, km, vm, og, gcol, grow, g_h):
    B, L, W = qm.shape
    nh = W // ML_HD
    ch = min(L, ML_CHUNK)
    tok = lambda c: pl.BlockSpec((1, ch, c), lambda b, t: (b, t, 0))
    return pl.pallas_call(
        _mlstm_prompt_kernel,
        grid=(B, L // ch),
        in_specs=[tok(W), tok(W), tok(W), tok(W), tok(gcol.shape[-1]),
                  pl.BlockSpec((1, 2 * nh, ch), lambda b, t: (b, 0, t)),
                  _full(g_h.shape)],
        out_specs=[tok(W),
                   pl.BlockSpec((1, nh, ML_HD, ML_HD), lambda b, t: (b, 0, 0, 0)),
                   pl.BlockSpec((1, nh, 1, ML_HD), lambda b, t: (b, 0, 0, 0)),
                   pl.BlockSpec((1, nh, 1, 1), lambda b, t: (b, 0, 0, 0))],
        out_shape=[jax.ShapeDtypeStruct((B, L, W), BF16),
                   jax.ShapeDtypeStruct((B, nh, ML_HD, ML_HD), F32),
                   jax.ShapeDtypeStruct((B, nh, 1, ML_HD), F32),
                   jax.ShapeDtypeStruct((B, nh, 1, 1), F32)],
        scratch_shapes=[pltpu.VMEM((nh, ML_HD, ML_HD), F32),
                        pltpu.VMEM((nh, 1, ML_HD), F32),
                        pltpu.VMEM((nh, 1, 1), F32)],
        compiler_params=_params("parallel", "arbitrary"),
        name="mlstm_prompt",
    )(qm, km, vm, og, gcol, grow, g_h)


def _mlstm_sample_kernel(q_ref, k_ref, v_ref, og_ref, g_ref, gh_ref, c_ref, n_ref, m_ref,
                         h_ref, c_out, n_out, m_out):
    nh = c_ref.shape[1]
    d = c_ref.shape[-1]
    eye = (lax.broadcasted_iota(jnp.int32, (d, d), 0) == lax.broadcasted_iota(jnp.int32, (d, d), 1))

    def to_col(r):
        return jnp.sum(jnp.where(eye, r, 0.0), axis=1, keepdims=True)

    g = g_ref[0]
    for h in range(nh):
        sl = slice(h * d, (h + 1) * d)
        q = q_ref[0, :, sl].astype(F32)
        k = k_ref[0, :, sl].astype(F32)
        v = v_ref[0, :, sl].astype(F32)
        C = c_ref[0, h]
        n = n_ref[0, h]
        m = m_ref[0, h]
        ig = g[:, h:h + 1]
        lf = g[:, nh + h:nh + h + 1]
        log_c = lf + m
        m_t = jnp.maximum(log_c, ig)
        w_d = jnp.exp(ig - m_t)
        w_c = jnp.exp(log_c - m_t)
        s = jnp.sum(q * k, axis=-1, keepdims=True) * w_d
        cb = C.astype(BF16).astype(F32)
        qC = jnp.sum(to_col(q) * cb, axis=0, keepdims=True)
        num = s * v + w_c * qC
        den = s + w_c * jnp.sum(q * n, axis=-1, keepdims=True)
        hh = num / jnp.maximum(jnp.abs(den), jnp.exp(-m_t))
        c_out[0, h] = w_c * C + w_d * (to_col(k) * v)
        n_out[0, h] = w_c * n + w_d * k
        m_out[0, h] = m_t
        hn = _rms(hh, gh_ref[h:h + 1, :])
        h_ref[0, :, sl] = (og_ref[0, :, sl] * hn).astype(h_ref.dtype)


def _mlstm_sample(qm, km, vm, og, gcol, g_h, C, n, m):
    S, _, W = qm.shape
    nh = C.shape[1]
    d = C.shape[-1]
    seq = lambda *tail: pl.BlockSpec((1,) + tail, lambda b: (b,) + (0,) * len(tail))
    return pl.pallas_call(
        _mlstm_sample_kernel,
        grid=(S,),
        in_specs=[seq(1, W), seq(1, W), seq(1, W), seq(1, W), seq(1, gcol.shape[-1]), _full(g_h.shape),
                  seq(nh, d, d), seq(nh, 1, d), seq(nh, 1, 1)],
        out_specs=[seq(1, W), seq(nh, d, d), seq(nh, 1, d), seq(nh, 1, 1)],
        out_shape=[jax.ShapeDtypeStruct((S, 1, W), BF16),
                   jax.ShapeDtypeStruct(C.shape, F32),
                   jax.ShapeDtypeStruct(n.shape, F32),
                   jax.ShapeDtypeStruct(m.shape, F32)],
        compiler_params=_params("parallel"),
        name="mlstm_sample",
    )(qm, km, vm, og, gcol, g_h, C, n, m)


def _ln_swish(c, g, b):
    cc = c - jnp.mean(c, axis=-1, keepdims=True)
    y = cc * lax.rsqrt(jnp.mean(cc * cc, axis=-1, keepdims=True) + EPS) * g + b
    return y * _sigmoid(y)


CV_HALO = 32


def _conv_prompt_kernel(u_ref, halo_ref, w_ref, b_ref, g_ref, beta_ref, o_ref, win_ref):
    i = pl.program_id(1)
    tl = u_ref.shape[1]
    halo = halo_ref[0]
    win_ref[0:CV_HALO, :] = jnp.where(i == 0, jnp.zeros_like(halo), halo)
    win_ref[CV_HALO:, :] = u_ref[0]
    off = CV_HALO - (CV_K - 1)
    acc = jnp.zeros((tl, u_ref.shape[2]), F32) + b_ref[...]
    for j in range(CV_K):
        acc = acc + w_ref[j:j + 1, :] * win_ref[off + j:off + j + tl, :]
    o_ref[0] = _ln_swish(acc, g_ref[...], beta_ref[...]).astype(o_ref.dtype)


def _conv_prompt(u, w, b, g, beta):
    B, L, C = u.shape
    tl = min(L, 256)
    r = tl // CV_HALO
    return pl.pallas_call(
        _conv_prompt_kernel,
        grid=(B, L // tl),
        in_specs=[pl.BlockSpec((1, tl, C), lambda b_, i: (b_, i, 0)),
                  pl.BlockSpec((1, CV_HALO, C), lambda b_, i: (b_, jnp.maximum(i * r - 1, 0), 0)),
                  _full(w.shape), _full(b.shape), _full(g.shape), _full(beta.shape)],
        out_specs=pl.BlockSpec((1, tl, C), lambda b_, i: (b_, i, 0)),
        out_shape=jax.ShapeDtypeStruct((B, L, C), BF16),
        scratch_shapes=[pltpu.VMEM((tl + CV_HALO, C), F32)],
        compiler_params=_params("parallel", "parallel"),
        name="conv_prompt",
    )(u, u, w, b, g, beta)


def _conv_sample_kernel(xx_ref, w_ref, b_ref, g_ref, beta_ref, o_ref):
    xx = xx_ref[...]
    c = jnp.sum(xx * w_ref[...][None], axis=1) + b_ref[...]
    o_ref[...] = _ln_swish(c, g_ref[...], beta_ref[...]).astype(o_ref.dtype)


def _conv_sample(xx, w, b, g, beta):
    S, K, C = xx.shape
    nb = min(S, 8)
    return pl.pallas_call(
        _conv_sample_kernel,
        grid=(S // nb,),
        in_specs=[pl.BlockSpec((nb, K, C), lambda s: (s, 0, 0)),
                  _full(w.shape), _full(b.shape), _full(g.shape), _full(beta.shape)],
        out_specs=pl.BlockSpec((nb, C), lambda s: (s, 0)),
        out_shape=jax.ShapeDtypeStruct((S, C), BF16),
        compiler_params=_params("parallel"),
        name="conv_sample",
    )(xx, w, b, g, beta)


def _merge_kernel(x_ref, g_ref, wg_ref, hs_ref, hm_ref, hc_ref, ws_ref, wm_ref, wc_ref, wo_ref, o_ref):
    x = x_ref[...]
    D = x.shape[-1]
    xn = _rms(x, g_ref[...]).astype(BF16)
    merged = None
    for b, (h_ref, w_ref) in enumerate(((hs_ref, ws_ref), (hm_ref, wm_ref), (hc_ref, wc_ref))):
        gate = _sigmoid(_dot(xn, wg_ref[:, b * D:(b + 1) * D]))
        term = gate * _dot(h_ref[...], w_ref[...])
        merged = term if merged is None else merged + term
    o_ref[...] = x + _dot(merged.astype(BF16), wo_ref[...])


def _merge(x, h_sb, h_ml, h_cv, lw):
    T, D = x.shape
    tm = min(T, 256)
    W = h_sb.shape[1]
    tok = lambda c: pl.BlockSpec((tm, c), lambda i: (i, 0))
    weights = [lw["w_sb_o"], lw["w_ml_o"], lw["w_cv_o"], lw["w_out"]]
    return pl.pallas_call(
        _merge_kernel,
        grid=(T // tm,),
        in_specs=[tok(D), _full(lw["g_mix"].shape), _full(lw["w_gates"].shape), tok(W), tok(W), tok(W)]
                 + [_full(w.shape) for w in weights],
        out_specs=tok(D),
        out_shape=jax.ShapeDtypeStruct((T, D), F32),
        compiler_params=_params("parallel"),
        name="merge",
    )(x, lw["g_mix"], lw["w_gates"], h_sb, h_ml, h_cv, *weights)


def _norm_proj_kernel(x_ref, g_ref, w_ref, gh_ref, o_ref, *, head_norm):
    xn = _rms(x_ref[...], g_ref[...]).astype(BF16)
    z = _dot(xn, w_ref[...])
    if head_norm:
        for h in range(z.shape[1] // XA_HD):
            sl = slice(h * XA_HD, (h + 1) * XA_HD)
            o_ref[:, sl] = _rms(z[:, sl], gh_ref[...]).astype(o_ref.dtype)
    else:
        o_ref[...] = z.astype(o_ref.dtype)


def _norm_proj(x, g, w, gh, head_norm, out_dtype):
    T, D = x.shape
    N = w.shape[1]
    tm = min(T, 256)
    return pl.pallas_call(
        functools.partial(_norm_proj_kernel, head_norm=head_norm),
        grid=(T // tm,),
        in_specs=[pl.BlockSpec((tm, D), lambda i: (i, 0)), _full(g.shape), _full(w.shape), _full(gh.shape)],
        out_specs=pl.BlockSpec((tm, N), lambda i: (i, 0)),
        out_shape=jax.ShapeDtypeStruct((T, N), out_dtype),
        compiler_params=_params("parallel"),
        name="norm_proj",
    )(x, g, w, gh)


def _proj_res_kernel(x_ref, h_ref, w_ref, o_ref):
    o_ref[...] = x_ref[...] + _dot(h_ref[...], w_ref[...])


def _proj_res(x, h, w):
    T, D = x.shape
    tm = min(T, 256)
    return pl.pallas_call(
        _proj_res_kernel,
        grid=(T // tm,),
        in_specs=[pl.BlockSpec((tm, D), lambda i: (i, 0)),
                  pl.BlockSpec((tm, h.shape[1]), lambda i: (i, 0)), _full(w.shape)],
        out_specs=pl.BlockSpec((tm, D), lambda i: (i, 0)),
        out_shape=jax.ShapeDtypeStruct((T, D), F32),
        compiler_params=_params("parallel"),
        name="proj_res",
    )(x, h, w)


def _xa_prompt_kernel(q_ref, k_ref, v_ref, o_ref):
    k = k_ref[0].astype(BF16)
    v = v_ref[0].astype(BF16)
    for h in range(q_ref.shape[-1] // XA_HD):
        sl = slice(h * XA_HD, (h + 1) * XA_HD)
        s = _dot_nt(q_ref[0, :, sl], k[:, sl]) * (XA_HD ** -0.5)
        e = jnp.exp(s - jnp.max(s, axis=-1, keepdims=True))
        p = e / jnp.sum(e, axis=-1, keepdims=True)
        o_ref[0, :, sl] = _dot(p.astype(BF16), v[:, sl]).astype(o_ref.dtype)


def _xa_prompt(q, mk, mv):
    B, L, W = q.shape
    M = mk.shape[1]
    tm = min(L, 512)
    return pl.pallas_call(
        _xa_prompt_kernel,
        grid=(B, L // tm),
        in_specs=[pl.BlockSpec((1, tm, W), lambda b, i: (b, i, 0)),
                  pl.BlockSpec((1, M, W), lambda b, i: (b, 0, 0)),
                  pl.BlockSpec((1, M, W), lambda b, i: (b, 0, 0))],
        out_specs=pl.BlockSpec((1, tm, W), lambda b, i: (b, i, 0)),
        out_shape=jax.ShapeDtypeStruct((B, L, W), BF16),
        compiler_params=_params("parallel", "parallel"),
        name="xa_prompt",
    )(q, mk, mv)


def _xa_sample_kernel(q_ref, k_ref, v_ref, o_ref):
    q = q_ref[0].astype(F32)
    k = k_ref[...].astype(BF16).astype(F32)
    v = v_ref[...].astype(BF16).astype(F32)
    s = jnp.sum(k * q[None], axis=-1, keepdims=True) * (XA_HD ** -0.5)
    e = jnp.exp(s - jnp.max(s, axis=0, keepdims=True))
    p = e / jnp.sum(e, axis=0, keepdims=True)
    p = p.astype(BF16).astype(F32)
    o_ref[0] = jnp.sum(p * v, axis=0).astype(o_ref.dtype)


def _xa_sample(layer, q, cache_k, cache_v):
    S, nh, hd = q.shape
    M = cache_k.shape[2]
    mem = pl.BlockSpec((None, None, M, nh, hd), lambda b: (layer, b, 0, 0, 0))
    return pl.pallas_call(
        _xa_sample_kernel,
        grid=(S,),
        in_specs=[pl.BlockSpec((1, nh, hd), lambda b: (b, 0, 0)), mem, mem],
        out_specs=pl.BlockSpec((1, nh, hd), lambda b: (b, 0, 0)),
        out_shape=jax.ShapeDtypeStruct((S, nh, hd), BF16),
        compiler_params=_params("parallel"),
        name="xa_sample",
    )(q, cache_k, cache_v)


NOT_TAKEN = 99.0


def _take_top(s, n, exact):
    idx = lax.broadcasted_iota(jnp.int32, s.shape, 0)
    big = jnp.int32(s.shape[0])
    slot = lax.broadcasted_iota(jnp.int32, (n, s.shape[1]), 0)
    vals = jnp.zeros((n, s.shape[1]), F32)
    rank = jnp.full(s.shape, NOT_TAKEN, F32)
    for r in range(n):
        mx = jnp.max(s, axis=0, keepdims=True)
        if exact:
            first = jnp.min(jnp.where(s == mx, idx, big), axis=0, keepdims=True)
            hit = idx == first
        else:
            hit = s == mx
        s = jnp.where(hit, NEG_INF, s)
        rank = jnp.where(hit, float(r), rank)
        vals = jnp.where(slot == r, mx, vals)
    excess = jnp.sum(jnp.where(rank < NOT_TAKEN, 1.0, 0.0), axis=0, keepdims=True) - float(n)
    return vals, rank, excess


def _cand_pieces(v1, v2):
    sub = lax.broadcasted_iota(jnp.int32, (8, v1.shape[1]), 0)
    pieces = [v1[0:1] + v2]
    for i in range(1, 8):
        nj = PK_TOPK // (i + 1)
        pieces.append(jnp.where(sub < nj, v1[i:i + 1] + v2[0:8], NEG_INF))
    pieces.append(v1[8:16] + v2[0:1])
    return jnp.concatenate(pieces, axis=0)


def _peer_select(x_ref, g_ref, wq_ref, k1h_ref, k1l_ref, k2h_ref, k2l_ref,
                 xt_ref, na_ref, r2_ref, e1_ref, e2_ref, exact):
    xn = _rms(x_ref[...], g_ref[...])
    xt_ref[...] = xn.T.astype(BF16)
    q = _dot(xn.astype(BF16), wq_ref[...])
    nheads = k1h_ref.shape[0]
    half = k1h_ref.shape[2]
    ties = jnp.zeros((1, q.shape[0]), F32)
    for h in range(nheads):
        q1h, q1l = _split2(q[:, (2 * h) * half:(2 * h + 1) * half])
        q2h, q2l = _split2(q[:, (2 * h + 1) * half:(2 * h + 2) * half])
        s1 = _dot_nt(k1h_ref[h], q1h) + _dot_nt(k1l_ref[h], q1h) + _dot_nt(k1h_ref[h], q1l)
        s2 = _dot_nt(k2h_ref[h], q2h) + _dot_nt(k2l_ref[h], q2h) + _dot_nt(k2h_ref[h], q2l)
        v1, r1, x1 = _take_top(s1, PK_TOPK, exact)
        v2, r2, x2 = _take_top(s2, PK_TOPK, exact)
        top, rc, xc = _take_top(_cand_pieces(v1, v2), PK_TOPK, exact)
        ties = jnp.maximum(ties, jnp.maximum(jnp.maximum(x1, x2), xc))
        zsum = jnp.sum(jnp.exp(top - top[0:1]), axis=0, keepdims=True)
        sel = jnp.where(rc < NOT_TAKEN, 1.0, 0.0)
        na = jnp.zeros_like(s1)
        for i in range(PK_TOPK):
            if i == 0:
                ni = jnp.sum(sel[0:PK_TOPK], axis=0, keepdims=True)
            elif i < 8:
                ni = jnp.sum(sel[PK_TOPK + 8 * (i - 1):PK_TOPK + 8 * i], axis=0, keepdims=True)
            else:
                ni = sel[PK_TOPK + 48 + i:PK_TOPK + 49 + i]
            na = jnp.where(r1 == float(i), ni, na)
        na_ref[h] = na
        r2_ref[h] = r2
        e1_ref[h] = jnp.where(r1 < NOT_TAKEN, jnp.exp(s1 - v1[0:1]) / zsum, 0.0)
        e2_ref[h] = jnp.where(r2 < NOT_TAKEN, jnp.exp(s2 - v2[0:1]), 0.0)
    return jnp.max(ties)


def _peer_topk_kernel(*refs):
    ties = _peer_select(*refs, exact=False)

    @pl.when(ties > 0.0)
    def _():
        _peer_select(*refs, exact=True)


def _peer_topk(x, lw):
    T, D = x.shape
    tm = min(T, 256)
    nheads, nkeys, _ = lw["pk_k1_h"].shape
    weights = [lw["g_ffn"], lw["w_pq"], lw["pk_k1_h"], lw["pk_k1_l"], lw["pk_k2_h"], lw["pk_k2_l"]]
    key_arr = lambda dt: jax.ShapeDtypeStruct((nheads, nkeys, T), dt)
    key_spec = pl.BlockSpec((nheads, nkeys, tm), lambda i: (0, 0, i))
    return pl.pallas_call(
        _peer_topk_kernel,
        grid=(T // tm,),
        in_specs=[pl.BlockSpec((tm, D), lambda i: (i, 0))] + [_full(w.shape) for w in weights],
        out_specs=[pl.BlockSpec((D, tm), lambda i: (0, i))] + [key_spec] * 4,
        out_shape=[jax.ShapeDtypeStruct((D, T), BF16)] + [key_arr(F32)] * 4,
        compiler_params=_params("parallel"),
        name="peer_topk",
    )(x, *weights)


def _gelu(x):
    return 0.5 * x * (1.0 + lax.erf(x * (2.0 ** -0.5)))


PEER_ROWS = 32
PEER_AGROUP = 4
PEER_MM_ROWS = 256


def _peer_dense_kernel(x_ref, xt_ref, u_ref, vt_ref, na_ref, r2_ref, e1_ref, e2_ref,
                       o_ref, acc_ref, act_ref, gate_ref, w_ref):
    j = pl.program_id(1)
    last = pl.num_programs(1) - 1
    nheads, nkeys, tm = r2_ref.shape
    na = na_ref.shape[1]
    slot = j % 2

    @pl.when(j == 0)
    def _():
        acc_ref[...] = jnp.zeros_like(acc_ref)
        w_ref[1] = jnp.zeros(w_ref.shape[1:], w_ref.dtype)

    cw = min(tm, LANES)
    mw = min(tm, 2 * LANES)
    sub = 8
    blk = (PEER_ROWS // sub, sub, cw)

    def apply_values(rows, cols):
        acc_ref[rows, cols] += _dot(vt_ref[rows, :], w_ref[1 - slot, :, cols])

    def activations(rows, cols):
        act_ref[rows, cols] = _dot(u_ref[rows, :], xt_ref[:, cols])

    def gate_piece(c, bq, ag):
        ln = slice(c * cw, (c + 1) * cw)
        rb = slice(bq * PEER_ROWS, (bq + 1) * PEER_ROWS)
        g = [jnp.zeros(blk, F32) for _ in range(PEER_AGROUP)]
        for h in range(nheads):
            r2 = r2_ref[h, rb, ln].reshape(blk)
            e2 = e2_ref[h, rb, ln].reshape(blk)
            for k in range(PEER_AGROUP):
                n_a = jnp.broadcast_to(na_ref[h, ag + k:ag + k + 1, ln], (sub, cw))[None]
                e1 = jnp.broadcast_to(e1_ref[h, ag + k:ag + k + 1, ln], (sub, cw))[None]
                g[k] = g[k] + jnp.where(r2 < n_a, e1 * e2, 0.0)
        for k in range(PEER_AGROUP):
            r0 = (ag + k) * nkeys + bq * PEER_ROWS
            gate_ref[r0:r0 + PEER_ROWS, ln] = g[k].reshape(PEER_ROWS, cw)

    te, D = u_ref.shape
    mm = []
    for m in range(tm // mw):
        cols = slice(m * mw, (m + 1) * mw)
        for r in range(0, D, PEER_MM_ROWS):
            mm.append(functools.partial(apply_values, slice(r, r + PEER_MM_ROWS), cols))
        for r in range(0, te, PEER_MM_ROWS):
            mm.append(functools.partial(activations, slice(r, r + PEER_MM_ROWS), cols))
    gates = [functools.partial(gate_piece, c, bq, ag) for c in range(tm // cw)
             for bq in range(nkeys // PEER_ROWS) for ag in range(0, na, PEER_AGROUP)]
    per = -(-len(gates) // len(mm))
    for i, piece in enumerate(mm):
        piece()
        for gp in gates[i * per:(i + 1) * per]:
            gp()
    for gp in gates[len(mm) * per:]:
        gp()
    w_ref[slot] = (gate_ref[...] * _gelu(act_ref[...])).astype(BF16)

    @pl.when(j == last)
    def _():
        o_ref[...] = x_ref[...] + acc_ref[...].T


def _peer_dense(x, xt, na_, r2, e1, e2, lw):
    T, D = x.shape
    u, vt = lw["pk_u"], lw["pk_vt"]
    E = u.shape[0]
    nheads, nkeys, _ = r2.shape
    tm = min(T, 512)
    na = 8
    te = na * nkeys
    nt = E // te
    cur = lambda j: jnp.minimum(j, nt - 1)
    key_spec = pl.BlockSpec((nheads, nkeys, tm), lambda i, j: (0, 0, i))
    a_spec = pl.BlockSpec((nheads, na, tm), lambda i, j: (0, cur(j), i))
    return pl.pallas_call(
        _peer_dense_kernel,
        grid=(T // tm, nt + 1),
        in_specs=[pl.BlockSpec((tm, D), lambda i, j: (i, 0)),
                  pl.BlockSpec((D, tm), lambda i, j: (0, i)),
                  pl.BlockSpec((te, D), lambda i, j: (cur(j), 0)),
                  pl.BlockSpec((D, te), lambda i, j: (0, jnp.maximum(j - 1, 0))),
                  a_spec, key_spec, a_spec, key_spec],
        out_specs=pl.BlockSpec((tm, D), lambda i, j: (i, 0)),
        out_shape=jax.ShapeDtypeStruct((T, D), F32),
        scratch_shapes=[pltpu.VMEM((D, tm), F32), pltpu.VMEM((te, tm), F32), pltpu.VMEM((te, tm), F32),
                        pltpu.VMEM((2, te, tm), BF16)],
        compiler_params=_params("parallel", "arbitrary"),
        name="peer_dense",
    )(x, xt, u, vt, na_, r2, e1, e2)


def _prep_weights(l, p):
    w_in = p["w_in"][l]
    D = w_in.shape[0]
    sbw = p["w_sb_o"].shape[1]
    mlw = p["w_ml_o"].shape[1]
    cvw = p["w_cv_o"].shape[1]
    nh = p["ml_ibias"].shape[1]
    o = 0
    w_sb = w_in[:, o:o + 3 * sbw]; o += 3 * sbw
    w_ml = w_in[:, o:o + 4 * mlw]; o += 4 * mlw
    w_if = w_in[:, o:o + 2 * nh]; o += 2 * nh
    w_ga = w_in[:, o:o + cvw]; o += cvw
    w_gb = w_in[:, o:o + cvw]; o += cvw
    w_gates = w_in[:, o:]
    w_if_pad = jnp.pad(w_if, ((0, 0), (0, LANES - 2 * nh)))
    w_if_h, w_if_l = _split2(w_if_pad)
    w_ift_h, w_ift_l = _split2(w_if.T)
    b_if = jnp.concatenate([p["ml_ibias"][l], p["ml_fbias"][l]])
    idx = jnp.arange(sbw) // SB_HD
    k1h, k1l = _split2(p["pk_k1"][l])
    k2h, k2l = _split2(p["pk_k2"][l])
    row = lambda a: a.reshape(1, -1)
    return {
        "g_mix": row(p["g_mix"][l]), "w_sb": w_sb.astype(BF16), "w_ml": w_ml.astype(BF16),
        "bd64": (idx[:, None] == idx[None, :]).astype(BF16),
        "g_sb_q": row(jnp.tile(p["g_sb_q"][l], sbw // SB_HD)),
        "g_sb_k": row(jnp.tile(p["g_sb_k"][l], sbw // SB_HD)),
        "w_if_h": w_if_h, "w_if_l": w_if_l, "b_if": row(jnp.pad(b_if, (0, LANES - 2 * nh))),
        "w_ift_h": w_ift_h, "w_ift_l": w_ift_l, "b_ift": b_if.reshape(-1, 1),
        "w_ga": w_ga.astype(BF16), "w_gb": w_gb.astype(BF16), "w_gates": w_gates.astype(BF16),
        "sb_bias": p["sb_bias"][l], "g_ml_h": p["g_ml_h"][l],
        "cv_w": p["cv_w"][l], "cv_b": row(p["cv_b"][l]),
        "cv_ln_g": row(p["cv_ln_g"][l]), "cv_ln_b": row(p["cv_ln_b"][l]),
        "w_sb_o": p["w_sb_o"][l].astype(BF16), "w_ml_o": p["w_ml_o"][l].astype(BF16),
        "w_cv_o": p["w_cv_o"][l].astype(BF16), "w_out": p["w_out"][l].astype(BF16),
        "g_xa": row(p["g_xa"][l]), "g_mem": row(p["g_mem"][l]),
        "w_xq": p["w_xq"][l].astype(BF16), "w_xk": p["w_xk"][l].astype(BF16),
        "w_xv": p["w_xv"][l].astype(BF16), "g_xq": row(p["g_xq"][l]), "g_xk": row(p["g_xk"][l]),
        "w_xo": p["w_xo"][l].astype(BF16),
        "g_ffn": row(p["g_ffn"][l]), "w_pq": p["w_pq"][l].astype(BF16),
        "pk_k1_h": k1h, "pk_k1_l": k1l, "pk_k2_h": k2h, "pk_k2_l": k2l,
        "pk_u": p["pk_u"][l].astype(BF16), "pk_vt": p["pk_v"][l].T.astype(BF16),
    }


def _tail(x, h_sb, h_ml, h_cv, lw, xa_core):
    x = _merge(x, h_sb, h_ml, h_cv, lw)
    q = _norm_proj(x, lw["g_xa"], lw["w_xq"], lw["g_xq"], True, BF16)
    x = _proj_res(x, xa_core(q), lw["w_xo"])
    xt, na, r2, e1, e2 = _peer_topk(x, lw)
    return _peer_dense(x, xt, na, r2, e1, e2, lw)


def kernel(x_prompt, x_sample, cache_sb_k, cache_sb_v, state_ml_C, state_ml_n, state_ml_m, state_conv, cache_mem_k, cache_mem_v, page_table, mem_prompt, g_mix, w_in, g_sb_q, g_sb_k, sb_bias, ml_ibias, ml_fbias, g_ml_h, cv_w, cv_b, cv_ln_g, cv_ln_b, w_sb_o, w_ml_o, w_cv_o, w_out, g_xa, g_mem, w_xq, w_xk, w_xv, g_xq, g_xk, w_xo, g_ffn, w_pq, pk_k1, pk_k2, pk_u, pk_v):
    p = dict(g_mix=g_mix, w_in=w_in, g_sb_q=g_sb_q, g_sb_k=g_sb_k, sb_bias=sb_bias, ml_ibias=ml_ibias,
             ml_fbias=ml_fbias, g_ml_h=g_ml_h, cv_w=cv_w, cv_b=cv_b, cv_ln_g=cv_ln_g, cv_ln_b=cv_ln_b,
             w_sb_o=w_sb_o, w_ml_o=w_ml_o, w_cv_o=w_cv_o, w_out=w_out, g_xa=g_xa, g_mem=g_mem, w_xq=w_xq,
             w_xk=w_xk, w_xv=w_xv, g_xq=g_xq, g_xk=g_xk, w_xo=w_xo, g_ffn=g_ffn, w_pq=w_pq,
             pk_k1=pk_k1, pk_k2=pk_k2, pk_u=pk_u, pk_v=pk_v)
    depth = w_in.shape[0]
    B, L, D = x_prompt.shape
    S = x_sample.shape[0]
    nh_sb = sb_bias.shape[1]
    nh_ml = ml_ibias.shape[1]
    sbw = w_sb_o.shape[1]
    M = mem_prompt.shape[1]
    xaw = w_xo.shape[1]
    n_pool, page = cache_sb_k.shape[1], cache_sb_k.shape[2]
    ck = cache_sb_k.reshape(depth, n_pool, page * nh_sb, SB_HD)
    cv = cache_sb_v.reshape(depth, n_pool, page * nh_sb, SB_HD)
    mem_flat = mem_prompt.reshape(B * M, D)

    yp, ys = x_prompt, x_sample.reshape(1, S, D)
    outs = [[] for _ in range(14)]
    for l in range(depth):
        lw = _prep_weights(l, p)

        mk = _norm_proj(mem_flat, lw["g_mem"], lw["w_xk"], lw["g_xk"], True, F32).reshape(B, M, xaw)
        mv = _norm_proj(mem_flat, lw["g_mem"], lw["w_xv"], lw["g_xk"], False, F32).reshape(B, M, xaw)

        q, kf, kb, vf, vb, qm, km, vm, og, gcol, grow, u = _in_proj(yp, lw)
        h_sb = _sb_prompt(q, kb, vb, lw["sb_bias"])
        h_ml, Cp, n_p, m_p = _mlstm_prompt(qm, km, vm, og, gcol, grow, lw["g_ml_h"])
        h_cv = _conv_prompt(u, lw["cv_w"], lw["cv_b"], lw["cv_ln_g"], lw["cv_ln_b"])
        xa_p = lambda qq: _xa_prompt(qq.reshape(B, L, xaw), mk, mv).reshape(B * L, xaw)
        yp = _tail(yp.reshape(B * L, D), h_sb.reshape(B * L, -1), h_ml.reshape(B * L, -1),
                   h_cv.reshape(B * L, -1), lw, xa_p).reshape(B, L, D)
        hist_p = u[:, L - (CV_K - 1):, :]
        if L < CV_K - 1:
            hist_p = jnp.concatenate([jnp.zeros((B, CV_K - 1 - L, u.shape[2]), F32), u], axis=1)

        q_s, kf_s, _, vf_s, _, qm_s, km_s, vm_s, og_s, gcol_s, _, u_s = _in_proj(ys, lw)
        h_sb_s = _sb_sample(l, page_table, q_s.reshape(S, nh_sb, SB_HD), lw["sb_bias"].reshape(nh_sb, 1), ck, cv)
        per_seq = lambda a: a.reshape(S, 1, a.shape[-1])
        h_ml_s, Cs, n_s, m_s = _mlstm_sample(
            per_seq(qm_s), per_seq(km_s), per_seq(vm_s), per_seq(og_s), per_seq(gcol_s), lw["g_ml_h"],
            state_ml_C[l], state_ml_n[l].reshape(S, nh_ml, 1, ML_HD), state_ml_m[l].reshape(S, nh_ml, 1, 1))
        xx = jnp.concatenate([state_conv[l], u_s.reshape(S, 1, -1)], axis=1)
        h_cv_s = _conv_sample(xx, lw["cv_w"], lw["cv_b"], lw["cv_ln_g"], lw["cv_ln_b"])
        xa_s = lambda qq: _xa_sample(l, qq.reshape(S, -1, XA_HD), cache_mem_k, cache_mem_v).reshape(S, xaw)
        ys = _tail(ys.reshape(S, D), h_sb_s.reshape(S, -1), h_ml_s.reshape(S, -1), h_cv_s, lw,
                   xa_s).reshape(1, S, D)

        vals = (kf.reshape(B, L, nh_sb, SB_HD), vf.reshape(B, L, nh_sb, SB_HD),
                kf_s.reshape(S, 1, nh_sb, SB_HD), vf_s.reshape(S, 1, nh_sb, SB_HD),
                Cp, n_p.reshape(B, nh_ml, ML_HD), m_p.reshape(B, nh_ml),
                Cs, n_s.reshape(S, nh_ml, ML_HD), m_s.reshape(S, nh_ml),
                hist_p, xx[:, 1:, :],
                mk.reshape(B, M, -1, XA_HD), mv.reshape(B, M, -1, XA_HD))
        for acc, v in zip(outs, vals):
            acc.append(v)
    return (yp, ys.reshape(S, 1, D)) + tuple(jnp.stack(a) for a in outs)
```

```python
import functools

import jax
import jax.numpy as jnp
from jax import lax
from jax.experimental import pallas as pl
from jax.experimental.pallas import tpu as pltpu

F32 = jnp.float32
BF16 = jnp.bfloat16
EPS = 1e-6
NEG_INF = float("-inf")

SB_HD = 64
ML_HD = 128
ML_CHUNK = 128
XA_HD = 128
CV_K = 31
PK_TOPK = 16
PK_NKEYS = 128
LANES = 128
VMEM_LIMIT = 56 * 1024 * 1024


def _params(*sem):
    return pltpu.CompilerParams(dimension_semantics=sem, vmem_limit_bytes=VMEM_LIMIT)


def _dot(a, b):
    return jnp.dot(a, b, preferred_element_type=F32)


def _dot_nt(a, b):
    return lax.dot_general(a, b, (((1,), (1,)), ((), ())), preferred_element_type=F32)


def _split2(x):
    hi = x.astype(BF16)
    lo = (x - hi.astype(F32)).astype(BF16)
    return hi, lo


def _split3(x):
    hi = x.astype(BF16)
    r = x - hi.astype(F32)
    mid = r.astype(BF16)
    lo = (r - mid.astype(F32)).astype(BF16)
    return hi, mid, lo


def _rms(x, g):
    ms = jnp.mean(x * x, axis=-1, keepdims=True)
    return x * lax.rsqrt(ms + EPS) * g


def _log_sigmoid(x):
    return jnp.minimum(x, 0.0) - jnp.log1p(jnp.exp(-jnp.abs(x)))


def _sigmoid(x):
    return 1.0 / (1.0 + jnp.exp(-x))


def _full(shape):
    n = len(shape)
    return pl.BlockSpec(shape, lambda *_: (0,) * n)


def _in_proj_kernel(x_ref, g_ref, wsb_ref, bd_ref, gq_ref, gk_ref, wml_ref,
                    wifh_ref, wifl_ref, bif_ref, wifth_ref, wiftl_ref, bift_ref,
                    wga_ref, wgb_ref,
                    q_ref, kf_ref, kb_ref, vf_ref, vb_ref, qm_ref, km_ref, vm_ref,
                    og_ref, gcol_ref, grow_ref, u_ref):
    x = x_ref[0]
    xn = _rms(x, g_ref[...])
    xh, xl = _split2(xn)

    sbw = q_ref.shape[-1]
    z = _dot(xh, wsb_ref[...])
    bd = bd_ref[...]

    def headnorm(zz, gain):
        sh, sl = _split2(zz * zz)
        ms = (_dot(sh, bd) + _dot(sl, bd)) * (1.0 / SB_HD)
        return zz * lax.rsqrt(ms + EPS) * gain

    qn = headnorm(z[:, :sbw], gq_ref[...])
    kn = headnorm(z[:, sbw:2 * sbw], gk_ref[...])
    vv = z[:, 2 * sbw:]
    q_ref[0] = (qn * (SB_HD ** -0.5)).astype(BF16)
    kf_ref[0] = kn
    kb_ref[0] = kn.astype(BF16)
    vf_ref[0] = vv
    vb_ref[0] = vv.astype(BF16)

    mlw = qm_ref.shape[-1]
    zm = _dot(xh, wml_ref[...])
    qm_ref[0] = zm[:, :mlw].astype(BF16)
    km_ref[0] = (zm[:, mlw:2 * mlw] * (ML_HD ** -0.5)).astype(BF16)
    vm_ref[0] = zm[:, 2 * mlw:3 * mlw].astype(BF16)
    og_ref[0] = _sigmoid(zm[:, 3 * mlw:])

    nh = grow_ref.shape[1] // 2
    wh, wl = wifh_ref[...], wifl_ref[...]
    zc = _dot(xh, wh) + _dot(xh, wl) + _dot(xl, wh) + bif_ref[...]
    lane = lax.broadcasted_iota(jnp.int32, zc.shape, 1)
    gcol_ref[0] = jnp.where(lane >= nh, _log_sigmoid(zc), zc)
    wth, wtl = wifth_ref[...], wiftl_ref[...]
    zr = _dot_nt(wth, xh) + _dot_nt(wtl, xh) + _dot_nt(wth, xl) + bift_ref[...]
    row = lax.broadcasted_iota(jnp.int32, zr.shape, 0)
    grow_ref[0] = jnp.where(row >= nh, _log_sigmoid(zr), zr)

    u_ref[0] = _dot(xh, wga_ref[...]) * _sigmoid(_dot(xh, wgb_ref[...]))


def _in_proj(x, lw):
    B, L, D = x.shape
    tm = min(L, 256)
    sbw = lw["w_sb"].shape[1] // 3
    mlw = lw["w_ml"].shape[1] // 4
    cvw = lw["w_ga"].shape[1]
    nh2 = lw["b_ift"].shape[0]
    tok = lambda c: pl.BlockSpec((1, tm, c), lambda b, i: (b, i, 0))
    weights = [lw["g_mix"], lw["w_sb"], lw["bd64"], lw["g_sb_q"], lw["g_sb_k"], lw["w_ml"],
               lw["w_if_h"], lw["w_if_l"], lw["b_if"], lw["w_ift_h"], lw["w_ift_l"], lw["b_ift"],
               lw["w_ga"], lw["w_gb"]]
    out_shape = [
        jax.ShapeDtypeStruct((B, L, sbw), BF16),
        jax.ShapeDtypeStruct((B, L, sbw), F32),
        jax.ShapeDtypeStruct((B, L, sbw), BF16),
        jax.ShapeDtypeStruct((B, L, sbw), F32),
        jax.ShapeDtypeStruct((B, L, sbw), BF16),
        jax.ShapeDtypeStruct((B, L, mlw), BF16),
        jax.ShapeDtypeStruct((B, L, mlw), BF16),
        jax.ShapeDtypeStruct((B, L, mlw), BF16),
        jax.ShapeDtypeStruct((B, L, mlw), F32),
        jax.ShapeDtypeStruct((B, L, LANES), F32),
        jax.ShapeDtypeStruct((B, nh2, L), F32),
        jax.ShapeDtypeStruct((B, L, cvw), F32),
    ]
    out_specs = [tok(sbw)] * 5 + [tok(mlw)] * 4 + [tok(LANES),
                 pl.BlockSpec((1, nh2, tm), lambda b, i: (b, 0, i)), tok(cvw)]
    return pl.pallas_call(
        _in_proj_kernel,
        grid=(B, L // tm),
        in_specs=[tok(D)] + [_full(w.shape) for w in weights],
        out_specs=out_specs,
        out_shape=out_shape,
        compiler_params=_params("parallel", "parallel"),
        name="in_proj",
    )(x, *weights)


def _split_trunc(x):
    hi = lax.bitcast_convert_type(lax.bitcast_convert_type(x, jnp.uint32) & jnp.uint32(0xFFFF0000), F32)
    return hi.astype(BF16), (x - hi).astype(BF16)


def _sb_logs(z, mask):
    l = jnp.log(1.0 + jnp.exp(-jnp.abs(z)))
    lb = jnp.minimum(z, 0.0) - l
    l1 = lb - z
    if mask is not None:
        l1 = jnp.where(mask, l1, 0.0)
    return lb, l1


SB_GROUP = 8


def _sb_prompt_kernel(bias_ref, q_ref, k_ref, v_ref, o_ref, *, tq):
    hg = pl.program_id(1)
    i = pl.program_id(2)
    pair = 2 * SB_HD
    lane = lax.broadcasted_iota(jnp.int32, (tq, pair), 1)
    qs, biases = [], []
    for h in range(SB_GROUP):
        qp = q_ref[0, :, (h // 2) * pair:(h // 2 + 1) * pair]
        keep = (lane < SB_HD) if h % 2 == 0 else (lane >= SB_HD)
        qs.append(jnp.where(keep, qp, jnp.zeros_like(qp)))
        biases.append(bias_ref[SB_GROUP * hg + h])
    row = lax.broadcasted_iota(jnp.int32, (tq, tq), 0)
    col = lax.broadcasted_iota(jnp.int32, (tq, tq), 1)
    tri = (row > col).astype(BF16)
    causal = col < row

    def step(j, carry, mask):
        start = pl.multiple_of(j * tq, tq)
        heads = range(SB_GROUP)
        cols = [slice((h // 2) * pair, (h // 2 + 1) * pair) for h in heads]
        z = [_dot_nt(qs[h], k_ref[0, pl.ds(start, tq), cols[h]]) + biases[h] for h in heads]
        logs = [_sb_logs(z[h], mask) for h in heads]
        parts = [_split_trunc(logs[h][1]) for h in heads]
        within = [_dot(parts[h][0], tri) + _dot(parts[h][1], tri) for h in heads]
        w = []
        for h in heads:
            wh = jnp.exp(logs[h][0] + within[h] + carry[h][1])
            w.append((jnp.where(mask, wh, 0.0) if mask is not None else wh).astype(BF16))
        out = []
        for h in heads:
            pv = _dot(w[h], v_ref[0, pl.ds(start, tq), cols[h]])
            c = carry[h][1] + within[h][:, 0:1] + logs[h][1][:, 0:1]
            out.append((carry[h][0] + pv, c))
        return tuple(out)

    init = tuple((jnp.zeros((tq, pair), F32), jnp.zeros((tq, 1), F32)) for _ in range(SB_GROUP))
    carry = step(i, init, causal)
    carry = lax.fori_loop(0, i, lambda t, cr: step(i - 1 - t, cr, None), carry)
    for p in range(SB_GROUP // 2):
        o_ref[0, :, p * pair:(p + 1) * pair] = jnp.where(
            lane < SB_HD, carry[2 * p][0], carry[2 * p + 1][0]).astype(o_ref.dtype)


def _sb_prompt(q, k, v, bias):
    B, L, W = q.shape
    tq = min(L, 256)
    gw = SB_GROUP * SB_HD
    return pl.pallas_call(
        functools.partial(_sb_prompt_kernel, tq=tq),
        grid_spec=pltpu.PrefetchScalarGridSpec(
            num_scalar_prefetch=1,
            grid=(B, W // gw, L // tq),
            in_specs=[pl.BlockSpec((1, tq, gw), lambda b, h, i, *_: (b, i, h)),
                      pl.BlockSpec((1, L, gw), lambda b, h, i, *_: (b, 0, h)),
                      pl.BlockSpec((1, L, gw), lambda b, h, i, *_: (b, 0, h))],
            out_specs=pl.BlockSpec((1, tq, gw), lambda b, h, i, *_: (b, i, h)),
        ),
        out_shape=jax.ShapeDtypeStruct((B, L, W), BF16),
        compiler_params=_params("parallel", "parallel", "parallel"),
        name="sb_prompt",
    )(bias, q, k, v)


SB_PAGES_PER_STEP = 8


def _bf16_round(x):
    return x.astype(BF16).astype(F32)


def _sb_sample_kernel(pt_ref, q_ref, bias_ref, *refs, pps):
    k_refs = refs[:pps]
    v_refs = refs[pps:2 * pps]
    o_ref, acc_ref, c_ref = refs[2 * pps:]
    g = pl.program_id(1)
    nh, hd, page = k_refs[0].shape
    bias = bias_ref[...]

    @pl.when(g == 0)
    def _():
        acc_ref[...] = jnp.zeros_like(acc_ref)
        c_ref[...] = jnp.zeros_like(c_ref)

    qb = q_ref[0] + jnp.zeros((nh, hd, page), F32)
    lane = lax.broadcasted_iota(jnp.int32, (nh, page), 1)
    pages = range(pps)
    z = [jnp.sum(_bf16_round(k_refs[t][...]) * qb, axis=1) + bias for t in pages]
    logs = [_sb_logs(z[t], None) for t in pages]
    incl = []
    for t in pages:
        x = logs[t][1]
        d = 1
        while d < page:
            x = x + jnp.where(lane < page - d, pltpu.roll(x, page - d, axis=1), 0.0)
            d *= 2
        incl.append(x)
    c = c_ref[...]
    acc = acc_ref[...]
    for t in pages:
        tail = incl[t] - logs[t][1] + c
        w = _bf16_round(jnp.exp(logs[t][0] + tail))
        acc = acc + w[:, None, :] * _bf16_round(v_refs[t][...])
        c = c + incl[t][:, 0:1]
    acc_ref[...] = acc
    c_ref[...] = c

    @pl.when(g == pl.num_programs(1) - 1)
    def _():
        ones = jnp.ones((8, page), BF16)
        o = sum(_dot_nt(ones, part) for part in _split3(acc.reshape(nh * hd, page)))
        o_ref[0] = o[0:1].astype(o_ref.dtype)


def _sb_sample(layer, page_table, q, bias_col, cache_k, cache_v):
    S, n_pages = page_table.shape
    _, _, nh, hd, page = cache_k.shape
    pps = min(SB_PAGES_PER_STEP, n_pages)

    def page_spec(t):
        return pl.BlockSpec((None, None, nh, hd, page),
                            lambda b, g, pt: (layer, pt[b, n_pages - 1 - (g * pps + t)], 0, 0, 0))

    return pl.pallas_call(
        functools.partial(_sb_sample_kernel, pps=pps),
        grid_spec=pltpu.PrefetchScalarGridSpec(
            num_scalar_prefetch=1,
            grid=(S, n_pages // pps),
            in_specs=[pl.BlockSpec((1, nh, hd, 1), lambda b, g, pt: (b, 0, 0, 0)),
                      pl.BlockSpec((nh, 1), lambda b, g, pt: (0, 0))]
                     + [page_spec(t) for t in range(pps)] * 2,
            out_specs=pl.BlockSpec((1, 1, nh * hd), lambda b, g, pt: (b, 0, 0)),
            scratch_shapes=[pltpu.VMEM((nh, hd, page), F32), pltpu.VMEM((nh, 1), F32)],
        ),
        out_shape=jax.ShapeDtypeStruct((S, 1, nh * hd), BF16),
        compiler_params=_params("parallel", "arbitrary"),
        name="sb_sample",
    )(page_table, q, bias_col, *([cache_k] * pps), *([cache_v] * pps))


def _mlstm_chunk_math(q, k, v, ig_col, ig_row, bcum_col, bcum_row, C, n, m):
    L = q.shape[0]
    row = lax.broadcasted_iota(jnp.int32, (L, L), 0)
    col = lax.broadcasted_iota(jnp.int32, (L, L), 1)
    log_d = jnp.where(col <= row, bcum_col - bcum_row + ig_row, NEG_INF)
    log_c = bcum_col + m
    m_t = jnp.maximum(log_c, jnp.max(log_d, axis=-1, keepdims=True))
    w_d = jnp.exp(log_d - m_t)
    w_c = jnp.exp(log_c - m_t)
    s = _dot_nt(q, k) * w_d
    num = _dot(s.astype(BF16), v) + w_c * _dot(q, C.astype(BF16))
    qf = q.astype(F32)
    den = jnp.sum(s, axis=-1, keepdims=True) + w_c * jnp.sum(qf * n, axis=-1, keepdims=True)
    h = num / jnp.maximum(jnp.abs(den), jnp.exp(-m_t))
    b_end = bcum_col[L - 1:L, :]
    log_in = b_end - bcum_col + ig_col
    m_new = jnp.maximum(b_end + m, jnp.max(log_in, axis=0, keepdims=True))
    w_in = jnp.exp(log_in - m_new)
    decay = jnp.exp(b_end + m - m_new)
    kw = k.astype(F32) * w_in
    C_new = decay * C + _dot(kw.T.astype(BF16), v)
    n_new = decay * n + jnp.sum(kw, axis=0, keepdims=True)
    return h, C_new, n_new, m_new


def _mlstm_prompt_kernel(q_ref, k_ref, v_ref, og_ref, gcol_ref, grow_ref, gh_ref,
                         h_ref, c_out, n_out, m_out, c_scr, n_scr, m_scr):
    t = pl.program_id(1)
    nh = c_scr.shape[0]
    L = q_ref.shape[1]

    @pl.when(t == 0)
    def _():
        c_scr[...] = jnp.zeros_like(c_scr)
        n_scr[...] = jnp.zeros_like(n_scr)
        m_scr[...] = jnp.zeros_like(m_scr)

    row = lax.broadcasted_iota(jnp.int32, (L, L), 0)
    col = lax.broadcasted_iota(jnp.int32, (L, L), 1)
    lower = (col <= row).astype(BF16)
    upper = (row <= col).astype(BF16)
    gcol = gcol_ref[0]
    grow = grow_ref[0]
    cum_col = sum(_dot(lower, part) for part in _split3(gcol))
    cum_row = sum(_dot(part, upper) for part in _split3(grow))
    for h in range(nh):
        sl = slice(h * ML_HD, (h + 1) * ML_HD)
        q, k, v = q_ref[0, :, sl], k_ref[0, :, sl], v_ref[0, :, sl]
        hh, C, n, m = _mlstm_chunk_math(
            q, k, v, gcol[:, h:h + 1], grow[h:h + 1, :],
            cum_col[:, nh + h:nh + h + 1], cum_row[nh + h:nh + h + 1, :],
            c_scr[h], n_scr[h], m_scr[h])
        c_scr[h] = C
        n_scr[h] = n
        m_scr[h] = m
        hn = _rms(hh, gh_ref[h:h + 1, :])
        h_ref[0, :, sl] = (og_ref[0, :, sl] * hn).astype(h_ref.dtype)

    @pl.when(t == pl.num_programs(1) - 1)
    def _():
        c_out[0] = c_scr[...]
        n_out[0] = n_scr[...]
        m_out[0] = m_scr[...]


def _mlstm_prompt(qm, km, vm, og, gcol, grow, g_h):
    B, L, W = qm.shape
    nh = W // ML_HD
    ch = min(L, ML_CHUNK)
    tok = lambda c: pl.BlockSpec((1, ch, c), lambda b, t: (b, t, 0))
    return pl.pallas_call(
        _mlstm_prompt_kernel,
        grid=(B, L // ch),
        in_specs=[tok(W), tok(W), tok(W), tok(W), tok(gcol.shape[-1]),
                  pl.BlockSpec((1, 2 * nh, ch), lambda b, t: (b, 0, t)),
                  _full(g_h.shape)],
        out_specs=[tok(W),
                   pl.BlockSpec((1, nh, ML_HD, ML_HD), lambda b, t: (b, 0, 0, 0)),
                   pl.BlockSpec((1, nh, 1, ML_HD), lambda b, t: (b, 0, 0, 0)),
                   pl.BlockSpec((1, nh, 1, 1), lambda b, t: (b, 0, 0, 0))],
        out_shape=[jax.ShapeDtypeStruct((B, L, W), BF16),
                   jax.ShapeDtypeStruct((B, nh, ML_HD, ML_HD), F32),
                   jax.ShapeDtypeStruct((B, nh, 1, ML_HD), F32),
                   jax.ShapeDtypeStruct((B, nh, 1, 1), F32)],
        scratch_shapes=[pltpu.VMEM((nh, ML_HD, ML_HD), F32),
                        pltpu.VMEM((nh, 1, ML_HD), F32),
                        pltpu.VMEM((nh, 1, 1), F32)],
        compiler_params=_params("parallel", "arbitrary"),
        name="mlstm_prompt",
    )(qm, km, vm, og, gcol, grow, g_h)


def _mlstm_sample_kernel(q_ref, k_ref, v_ref, og_ref, g_ref, gh_ref, c_ref, n_ref, m_ref,
                         h_ref, c_out, n_out, m_out):
    nh = c_ref.shape[1]
    d = c_ref.shape[-1]
    eye = (lax.broadcasted_iota(jnp.int32, (d, d), 0) == lax.broadcasted_iota(jnp.int32, (d, d), 1))

    def to_col(r):
        return jnp.sum(jnp.where(eye, r, 0.0), axis=1, keepdims=True)

    g = g_ref[0]
    for h in range(nh):
        sl = slice(h * d, (h + 1) * d)
        q = q_ref[0, :, sl].astype(F32)
        k = k_ref[0, :, sl].astype(F32)
        v = v_ref[0, :, sl].astype(F32)
        C = c_ref[0, h]
        n = n_ref[0, h]
        m = m_ref[0, h]
        ig = g[:, h:h + 1]
        lf = g[:, nh + h:nh + h + 1]
        log_c = lf + m
        m_t = jnp.maximum(log_c, ig)
        w_d = jnp.exp(ig - m_t)
        w_c = jnp.exp(log_c - m_t)
        s = jnp.sum(q * k, axis=-1, keepdims=True) * w_d
        cb = C.astype(BF16).astype(F32)
        qC = jnp.sum(to_col(q) * cb, axis=0, keepdims=True)
        num = s * v + w_c * qC
        den = s + w_c * jnp.sum(q * n, axis=-1, keepdims=True)
        hh = num / jnp.maximum(jnp.abs(den), jnp.exp(-m_t))
        c_out[0, h] = w_c * C + w_d * (to_col(k) * v)
        n_out[0, h] = w_c * n + w_d * k
        m_out[0, h] = m_t
        hn = _rms(hh, gh_ref[h:h + 1, :])
        h_ref[0, :, sl] = (og_ref[0, :, sl] * hn).astype(h_ref.dtype)


def _mlstm_sample(qm, km, vm, og, gcol, g_h, C, n, m):
    S, _, W = qm.shape
    nh = C.shape[1]
    d = C.shape[-1]
    seq = lambda *tail: pl.BlockSpec((1,) + tail, lambda b: (b,) + (0,) * len(tail))
    return pl.pallas_call(
        _mlstm_sample_kernel,
        grid=(S,),
        in_specs=[seq(1, W), seq(1, W), seq(1, W), seq(1, W), seq(1, gcol.shape[-1]), _full(g_h.shape),
                  seq(nh, d, d), seq(nh, 1, d), seq(nh, 1, 1)],
        out_specs=[seq(1, W), seq(nh, d, d), seq(nh, 1, d), seq(nh, 1, 1)],
        out_shape=[jax.ShapeDtypeStruct((S, 1, W), BF16),
                   jax.ShapeDtypeStruct(C.shape, F32),
                   jax.ShapeDtypeStruct(n.shape, F32),
                   jax.ShapeDtypeStruct(m.shape, F32)],
        compiler_params=_params("parallel"),
        name="mlstm_sample",
    )(qm, km, vm, og, gcol, g_h, C, n, m)


def _ln_swish(c, g, b):
    cc = c - jnp.mean(c, axis=-1, keepdims=True)
    y = cc * lax.rsqrt(jnp.mean(cc * cc, axis=-1, keepdims=True) + EPS) * g + b
    return y * _sigmoid(y)


CV_HALO = 32


def _conv_prompt_kernel(u_ref, halo_ref, w_ref, b_ref, g_ref, beta_ref, o_ref, win_ref):
    i = pl.program_id(1)
    tl = u_ref.shape[1]
    halo = halo_ref[0]
    win_ref[0:CV_HALO, :] = jnp.where(i == 0, jnp.zeros_like(halo), halo)
    win_ref[CV_HALO:, :] = u_ref[0]
    off = CV_HALO - (CV_K - 1)
    acc = jnp.zeros((tl, u_ref.shape[2]), F32) + b_ref[...]
    for j in range(CV_K):
        acc = acc + w_ref[j:j + 1, :] * win_ref[off + j:off + j + tl, :]
    o_ref[0] = _ln_swish(acc, g_ref[...], beta_ref[...]).astype(o_ref.dtype)


def _conv_prompt(u, w, b, g, beta):
    B, L, C = u.shape
    tl = min(L, 256)
    r = tl // CV_HALO
    return pl.pallas_call(
        _conv_prompt_kernel,
        grid=(B, L // tl),
        in_specs=[pl.BlockSpec((1, tl, C), lambda b_, i: (b_, i, 0)),
                  pl.BlockSpec((1, CV_HALO, C), lambda b_, i: (b_, jnp.maximum(i * r - 1, 0), 0)),
                  _full(w.shape), _full(b.shape), _full(g.shape), _full(beta.shape)],
        out_specs=pl.BlockSpec((1, tl, C), lambda b_, i: (b_, i, 0)),
        out_shape=jax.ShapeDtypeStruct((B, L, C), BF16),
        scratch_shapes=[pltpu.VMEM((tl + CV_HALO, C), F32)],
        compiler_params=_params("parallel", "parallel"),
        name="conv_prompt",
    )(u, u, w, b, g, beta)


def _conv_sample_kernel(xx_ref, w_ref, b_ref, g_ref, beta_ref, o_ref):
    xx = xx_ref[...]
    c = jnp.sum(xx * w_ref[...][None], axis=1) + b_ref[...]
    o_ref[...] = _ln_swish(c, g_ref[...], beta_ref[...]).astype(o_ref.dtype)


def _conv_sample(xx, w, b, g, beta):
    S, K, C = xx.shape
    nb = min(S, 8)
    return pl.pallas_call(
        _conv_sample_kernel,
        grid=(S // nb,),
        in_specs=[pl.BlockSpec((nb, K, C), lambda s: (s, 0, 0)),
                  _full(w.shape), _full(b.shape), _full(g.shape), _full(beta.shape)],
        out_specs=pl.BlockSpec((nb, C), lambda s: (s, 0)),
        out_shape=jax.ShapeDtypeStruct((S, C), BF16),
        compiler_params=_params("parallel"),
        name="conv_sample",
    )(xx, w, b, g, beta)


def _merge_kernel(x_ref, g_ref, wg_ref, hs_ref, hm_ref, hc_ref, ws_ref, wm_ref, wc_ref, wo_ref, o_ref):
    x = x_ref[...]
    D = x.shape[-1]
    xn = _rms(x, g_ref[...]).astype(BF16)
    merged = None
    for b, (h_ref, w_ref) in enumerate(((hs_ref, ws_ref), (hm_ref, wm_ref), (hc_ref, wc_ref))):
        gate = _sigmoid(_dot(xn, wg_ref[:, b * D:(b + 1) * D]))
        term = gate * _dot(h_ref[...], w_ref[...])
        merged = term if merged is None else merged + term
    o_ref[...] = x + _dot(merged.astype(BF16), wo_ref[...])


def _merge(x, h_sb, h_ml, h_cv, lw):
    T, D = x.shape
    tm = min(T, 256)
    W = h_sb.shape[1]
    tok = lambda c: pl.BlockSpec((tm, c), lambda i: (i, 0))
    weights = [lw["w_sb_o"], lw["w_ml_o"], lw["w_cv_o"], lw["w_out"]]
    return pl.pallas_call(
        _merge_kernel,
        grid=(T // tm,),
        in_specs=[tok(D), _full(lw["g_mix"].shape), _full(lw["w_gates"].shape), tok(W), tok(W), tok(W)]
                 + [_full(w.shape) for w in weights],
        out_specs=tok(D),
        out_shape=jax.ShapeDtypeStruct((T, D), F32),
        compiler_params=_params("parallel"),
        name="merge",
    )(x, lw["g_mix"], lw["w_gates"], h_sb, h_ml, h_cv, *weights)


def _norm_proj_kernel(x_ref, g_ref, w_ref, gh_ref, o_ref, *, head_norm):
    xn = _rms(x_ref[...], g_ref[...]).astype(BF16)
    z = _dot(xn, w_ref[...])
    if head_norm:
        for h in range(z.shape[1] // XA_HD):
            sl = slice(h * XA_HD, (h + 1) * XA_HD)
            o_ref[:, sl] = _rms(z[:, sl], gh_ref[...]).astype(o_ref.dtype)
    else:
        o_ref[...] = z.astype(o_ref.dtype)


def _norm_proj(x, g, w, gh, head_norm, out_dtype):
    T, D = x.shape
    N = w.shape[1]
    tm = min(T, 256)
    return pl.pallas_call(
        functools.partial(_norm_proj_kernel, head_norm=head_norm),
        grid=(T // tm,),
        in_specs=[pl.BlockSpec((tm, D), lambda i: (i, 0)), _full(g.shape), _full(w.shape), _full(gh.shape)],
        out_specs=pl.BlockSpec((tm, N), lambda i: (i, 0)),
        out_shape=jax.ShapeDtypeStruct((T, N), out_dtype),
        compiler_params=_params("parallel"),
        name="norm_proj",
    )(x, g, w, gh)


def _proj_res_kernel(x_ref, h_ref, w_ref, o_ref):
    o_ref[...] = x_ref[...] + _dot(h_ref[...], w_ref[...])


def _proj_res(x, h, w):
    T, D = x.shape
    tm = min(T, 256)
    return pl.pallas_call(
        _proj_res_kernel,
        grid=(T // tm,),
        in_specs=[pl.BlockSpec((tm, D), lambda i: (i, 0)),
                  pl.BlockSpec((tm, h.shape[1]), lambda i: (i, 0)), _full(w.shape)],
        out_specs=pl.BlockSpec((tm, D), lambda i: (i, 0)),
        out_shape=jax.ShapeDtypeStruct((T, D), F32),
        compiler_params=_params("parallel"),
        name="proj_res",
    )(x, h, w)


def _xa_prompt_kernel(q_ref, k_ref, v_ref, o_ref):
    k = k_ref[0].astype(BF16)
    v = v_ref[0].astype(BF16)
    for h in range(q_ref.shape[-1] // XA_HD):
        sl = slice(h * XA_HD, (h + 1) * XA_HD)
        s = _dot_nt(q_ref[0, :, sl], k[:, sl]) * (XA_HD ** -0.5)
        e = jnp.exp(s - jnp.max(s, axis=-1, keepdims=True))
        p = e / jnp.sum(e, axis=-1, keepdims=True)
        o_ref[0, :, sl] = _dot(p.astype(BF16), v[:, sl]).astype(o_ref.dtype)


def _xa_prompt(q, mk, mv):
    B, L, W = q.shape
    M = mk.shape[1]
    tm = min(L, 512)
    return pl.pallas_call(
        _xa_prompt_kernel,
        grid=(B, L // tm),
        in_specs=[pl.BlockSpec((1, tm, W), lambda b, i: (b, i, 0)),
                  pl.BlockSpec((1, M, W), lambda b, i: (b, 0, 0)),
                  pl.BlockSpec((1, M, W), lambda b, i: (b, 0, 0))],
        out_specs=pl.BlockSpec((1, tm, W), lambda b, i: (b, i, 0)),
        out_shape=jax.ShapeDtypeStruct((B, L, W), BF16),
        compiler_params=_params("parallel", "parallel"),
        name="xa_prompt",
    )(q, mk, mv)


def _xa_sample_kernel(q_ref, k_ref, v_ref, o_ref):
    q = q_ref[0].astype(F32)
    k = k_ref[...].astype(BF16).astype(F32)
    v = v_ref[...].astype(BF16).astype(F32)
    s = jnp.sum(k * q[None], axis=-1, keepdims=True) * (XA_HD ** -0.5)
    e = jnp.exp(s - jnp.max(s, axis=0, keepdims=True))
    p = e / jnp.sum(e, axis=0, keepdims=True)
    p = p.astype(BF16).astype(F32)
    o_ref[0] = jnp.sum(p * v, axis=0).astype(o_ref.dtype)


def _xa_sample(layer, q, cache_k, cache_v):
    S, nh, hd = q.shape
    M = cache_k.shape[2]
    mem = pl.BlockSpec((None, None, M, nh, hd), lambda b: (layer, b, 0, 0, 0))
    return pl.pallas_call(
        _xa_sample_kernel,
        grid=(S,),
        in_specs=[pl.BlockSpec((1, nh, hd), lambda b: (b, 0, 0)), mem, mem],
        out_specs=pl.BlockSpec((1, nh, hd), lambda b: (b, 0, 0)),
        out_shape=jax.ShapeDtypeStruct((S, nh, hd), BF16),
        compiler_params=_params("parallel"),
        name="xa_sample",
    )(q, cache_k, cache_v)


NOT_TAKEN = 99.0


def _take_top(s, n, exact):
    idx = lax.broadcasted_iota(jnp.int32, s.shape, 0)
    big = jnp.int32(s.shape[0])
    slot = lax.broadcasted_iota(jnp.int32, (n, s.shape[1]), 0)
    vals = jnp.zeros((n, s.shape[1]), F32)
    rank = jnp.full(s.shape, NOT_TAKEN, F32)
    for r in range(n):
        mx = jnp.max(s, axis=0, keepdims=True)
        if exact:
            first = jnp.min(jnp.where(s == mx, idx, big), axis=0, keepdims=True)
            hit = idx == first
        else:
            hit = s == mx
        s = jnp.where(hit, NEG_INF, s)
        rank = jnp.where(hit, float(r), rank)
        vals = jnp.where(slot == r, mx, vals)
    excess = jnp.sum(jnp.where(rank < NOT_TAKEN, 1.0, 0.0), axis=0, keepdims=True) - float(n)
    return vals, rank, excess


def _cand_pieces(v1, v2):
    sub = lax.broadcasted_iota(jnp.int32, (8, v1.shape[1]), 0)
    pieces = [v1[0:1] + v2]
    for i in range(1, 8):
        nj = PK_TOPK // (i + 1)
        pieces.append(jnp.where(sub < nj, v1[i:i + 1] + v2[0:8], NEG_INF))
    pieces.append(v1[8:16] + v2[0:1])
    return jnp.concatenate(pieces, axis=0)


def _peer_select(x_ref, g_ref, wq_ref, k1h_ref, k1l_ref, k2h_ref, k2l_ref,
                 xt_ref, na_ref, r2_ref, e1_ref, e2_ref, exact):
    xn = _rms(x_ref[...], g_ref[...])
    xt_ref[...] = xn.T.astype(BF16)
    q = _dot(xn.astype(BF16), wq_ref[...])
    nheads = k1h_ref.shape[0]
    half = k1h_ref.shape[2]
    ties = jnp.zeros((1, q.shape[0]), F32)
    for h in range(nheads):
        q1h, q1l = _split2(q[:, (2 * h) * half:(2 * h + 1) * half])
        q2h, q2l = _split2(q[:, (2 * h + 1) * half:(2 * h + 2) * half])
        s1 = _dot_nt(k1h_ref[h], q1h) + _dot_nt(k1l_ref[h], q1h) + _dot_nt(k1h_ref[h], q1l)
        s2 = _dot_nt(k2h_ref[h], q2h) + _dot_nt(k2l_ref[h], q2h) + _dot_nt(k2h_ref[h], q2l)
        v1, r1, x1 = _take_top(s1, PK_TOPK, exact)
        v2, r2, x2 = _take_top(s2, PK_TOPK, exact)
        top, rc, xc = _take_top(_cand_pieces(v1, v2), PK_TOPK, exact)
        ties = jnp.maximum(ties, jnp.maximum(jnp.maximum(x1, x2), xc))
        zsum = jnp.sum(jnp.exp(top - top[0:1]), axis=0, keepdims=True)
        sel = jnp.where(rc < NOT_TAKEN, 1.0, 0.0)
        na = jnp.zeros_like(s1)
        for i in range(PK_TOPK):
            if i == 0:
                ni = jnp.sum(sel[0:PK_TOPK], axis=0, keepdims=True)
            elif i < 8:
                ni = jnp.sum(sel[PK_TOPK + 8 * (i - 1):PK_TOPK + 8 * i], axis=0, keepdims=True)
            else:
                ni = sel[PK_TOPK + 48 + i:PK_TOPK + 49 + i]
            na = jnp.where(r1 == float(i), ni, na)
        na_ref[h] = na
        r2_ref[h] = r2
        e1_ref[h] = jnp.where(r1 < NOT_TAKEN, jnp.exp(s1 - v1[0:1]) / zsum, 0.0)
        e2_ref[h] = jnp.where(r2 < NOT_TAKEN, jnp.exp(s2 - v2[0:1]), 0.0)
    return jnp.max(ties)


def _peer_topk_kernel(*refs):
    ties = _peer_select(*refs, exact=False)

    @pl.when(ties > 0.0)
    def _():
        _peer_select(*refs, exact=True)


def _peer_topk(x, lw):
    T, D = x.shape
    tm = min(T, 256)
    nheads, nkeys, _ = lw["pk_k1_h"].shape
    weights = [lw["g_ffn"], lw["w_pq"], lw["pk_k1_h"], lw["pk_k1_l"], lw["pk_k2_h"], lw["pk_k2_l"]]
    key_arr = lambda dt: jax.ShapeDtypeStruct((nheads, nkeys, T), dt)
    key_spec = pl.BlockSpec((nheads, nkeys, tm), lambda i: (0, 0, i))
    return pl.pallas_call(
        _peer_topk_kernel,
        grid=(T // tm,),
        in_specs=[pl.BlockSpec((tm, D), lambda i: (i, 0))] + [_full(w.shape) for w in weights],
        out_specs=[pl.BlockSpec((D, tm), lambda i: (0, i))] + [key_spec] * 4,
        out_shape=[jax.ShapeDtypeStruct((D, T), BF16)] + [key_arr(F32)] * 4,
        compiler_params=_params("parallel"),
        name="peer_topk",
    )(x, *weights)


def _gelu(x):
    return 0.5 * x * (1.0 + lax.erf(x * (2.0 ** -0.5)))


PEER_ROWS = 32
PEER_AGROUP = 4
PEER_MM_ROWS = 1024


def _peer_dense_kernel(x_ref, xt_ref, u_ref, vt_ref, na_ref, r2_ref, e1_ref, e2_ref,
                       o_ref, acc_ref, act_ref, gate_ref, w_ref):
    j = pl.program_id(1)
    last = pl.num_programs(1) - 1
    nheads, nkeys, tm = r2_ref.shape
    na = na_ref.shape[1]
    slot = j % 2

    @pl.when(j == 0)
    def _():
        acc_ref[...] = jnp.zeros_like(acc_ref)
        w_ref[1] = jnp.zeros(w_ref.shape[1:], w_ref.dtype)

    cw = min(tm, LANES)
    mw = min(tm, 2 * LANES)
    sub = 8
    blk = (PEER_ROWS // sub, sub, cw)

    def apply_values(rows, cols):
        acc_ref[rows, cols] += _dot(vt_ref[rows, :], w_ref[1 - slot, :, cols])

    def activations(rows, cols):
        act_ref[rows, cols] = _dot(u_ref[rows, :], xt_ref[:, cols])

    def gate_piece(c, bq, ag):
        ln = slice(c * cw, (c + 1) * cw)
        rb = slice(bq * PEER_ROWS, (bq + 1) * PEER_ROWS)
        g = [jnp.zeros(blk, F32) for _ in range(PEER_AGROUP)]
        for h in range(nheads):
            r2 = r2_ref[h, rb, ln].reshape(blk)
            e2 = e2_ref[h, rb, ln].reshape(blk)
            for k in range(PEER_AGROUP):
                n_a = jnp.broadcast_to(na_ref[h, ag + k:ag + k + 1, ln], (sub, cw))[None]
                e1 = jnp.broadcast_to(e1_ref[h, ag + k:ag + k + 1, ln], (sub, cw))[None]
                g[k] = g[k] + jnp.where(r2 < n_a, e1 * e2, 0.0)
        for k in range(PEER_AGROUP):
            r0 = (ag + k) * nkeys + bq * PEER_ROWS
            gate_ref[r0:r0 + PEER_ROWS, ln] = g[k].reshape(PEER_ROWS, cw)

    te, D = u_ref.shape
    mm = []
    for m in range(tm // mw):
        cols = slice(m * mw, (m + 1) * mw)
        for r in range(0, D, PEER_MM_ROWS):
            mm.append(functools.partial(apply_values, slice(r, r + PEER_MM_ROWS), cols))
        for r in range(0, te, PEER_MM_ROWS):
            mm.append(functools.partial(activations, slice(r, r + PEER_MM_ROWS), cols))
    gates = [functools.partial(gate_piece, c, bq, ag) for c in range(tm // cw)
             for bq in range(nkeys // PEER_ROWS) for ag in range(0, na, PEER_AGROUP)]
    per = -(-len(gates) // len(mm))
    for i, piece in enumerate(mm):
        piece()
        for gp in gates[i * per:(i + 1) * per]:
            gp()
    for gp in gates[len(mm) * per:]:
        gp()
    w_ref[slot] = (gate_ref[...] * _gelu(act_ref[...])).astype(BF16)

    @pl.when(j == last)
    def _():
        o_ref[...] = x_ref[...] + acc_ref[...].T


def _peer_dense(x, xt, na_, r2, e1, e2, lw):
    T, D = x.shape
    u, vt = lw["pk_u"], lw["pk_vt"]
    E = u.shape[0]
    nheads, nkeys, _ = r2.shape
    tm = min(T, 512)
    na = 8
    te = na * nkeys
    nt = E // te
    cur = lambda j: jnp.minimum(j, nt - 1)
    key_spec = pl.BlockSpec((nheads, nkeys, tm), lambda i, j: (0, 0, i))
    a_spec = pl.BlockSpec((nheads, na, tm), lambda i, j: (0, cur(j), i))
    return pl.pallas_call(
        _peer_dense_kernel,
        grid=(T // tm, nt + 1),
        in_specs=[pl.BlockSpec((tm, D), lambda i, j: (i, 0)),
                  pl.BlockSpec((D, tm), lambda i, j: (0, i)),
                  pl.BlockSpec((te, D), lambda i, j: (cur(j), 0)),
                  pl.BlockSpec((D, te), lambda i, j: (0, jnp.maximum(j - 1, 0))),
                  a_spec, key_spec, a_spec, key_spec],
        out_specs=pl.BlockSpec((tm, D), lambda i, j: (i, 0)),
        out_shape=jax.ShapeDtypeStruct((T, D), F32),
        scratch_shapes=[pltpu.VMEM((D, tm), F32), pltpu.VMEM((te, tm), F32), pltpu.VMEM((te, tm), F32),
                        pltpu.VMEM((2, te, tm), BF16)],
        compiler_params=_params("parallel", "arbitrary"),
        name="peer_dense",
    )(x, xt, u, vt, na_, r2, e1, e2)


def _prep_weights(l, p):
    w_in = p["w_in"][l]
    D = w_in.shape[0]
    sbw = p["w_sb_o"].shape[1]
    mlw = p["w_ml_o"].shape[1]
    cvw = p["w_cv_o"].shape[1]
    nh = p["ml_ibias"].shape[1]
    o = 0
    w_sb = w_in[:, o:o + 3 * sbw]; o += 3 * sbw
    w_ml = w_in[:, o:o + 4 * mlw]; o += 4 * mlw
    w_if = w_in[:, o:o + 2 * nh]; o += 2 * nh
    w_ga = w_in[:, o:o + cvw]; o += cvw
    w_gb = w_in[:, o:o + cvw]; o += cvw
    w_gates = w_in[:, o:]
    w_if_pad = jnp.pad(w_if, ((0, 0), (0, LANES - 2 * nh)))
    w_if_h, w_if_l = _split2(w_if_pad)
    w_ift_h, w_ift_l = _split2(w_if.T)
    b_if = jnp.concatenate([p["ml_ibias"][l], p["ml_fbias"][l]])
    idx = jnp.arange(sbw) // SB_HD
    k1h, k1l = _split2(p["pk_k1"][l])
    k2h, k2l = _split2(p["pk_k2"][l])
    row = lambda a: a.reshape(1, -1)
    return {
        "g_mix": row(p["g_mix"][l]), "w_sb": w_sb.astype(BF16), "w_ml": w_ml.astype(BF16),
        "bd64": (idx[:, None] == idx[None, :]).astype(BF16),
        "g_sb_q": row(jnp.tile(p["g_sb_q"][l], sbw // SB_HD)),
        "g_sb_k": row(jnp.tile(p["g_sb_k"][l], sbw // SB_HD)),
        "w_if_h": w_if_h, "w_if_l": w_if_l, "b_if": row(jnp.pad(b_if, (0, LANES - 2 * nh))),
        "w_ift_h": w_ift_h, "w_ift_l": w_ift_l, "b_ift": b_if.reshape(-1, 1),
        "w_ga": w_ga.astype(BF16), "w_gb": w_gb.astype(BF16), "w_gates": w_gates.astype(BF16),
        "sb_bias": p["sb_bias"][l], "g_ml_h": p["g_ml_h"][l],
        "cv_w": p["cv_w"][l], "cv_b": row(p["cv_b"][l]),
        "cv_ln_g": row(p["cv_ln_g"][l]), "cv_ln_b": row(p["cv_ln_b"][l]),
        "w_sb_o": p["w_sb_o"][l].astype(BF16), "w_ml_o": p["w_ml_o"][l].astype(BF16),
        "w_cv_o": p["w_cv_o"][l].astype(BF16), "w_out": p["w_out"][l].astype(BF16),
        "g_xa": row(p["g_xa"][l]), "g_mem": row(p["g_mem"][l]),
        "w_xq": p["w_xq"][l].astype(BF16), "w_xk": p["w_xk"][l].astype(BF16),
        "w_xv": p["w_xv"][l].astype(BF16), "g_xq": row(p["g_xq"][l]), "g_xk": row(p["g_xk"][l]),
        "w_xo": p["w_xo"][l].astype(BF16),
        "g_ffn": row(p["g_ffn"][l]), "w_pq": p["w_pq"][l].astype(BF16),
        "pk_k1_h": k1h, "pk_k1_l": k1l, "pk_k2_h": k2h, "pk_k2_l": k2l,
        "pk_u": p["pk_u"][l].astype(BF16), "pk_vt": p["pk_v"][l].T.astype(BF16),
    }


def _tail(x, h_sb, h_ml, h_cv, lw, xa_core):
    x = _merge(x, h_sb, h_ml, h_cv, lw)
    q = _norm_proj(x, lw["g_xa"], lw["w_xq"], lw["g_xq"], True, BF16)
    x = _proj_res(x, xa_core(q), lw["w_xo"])
    xt, na, r2, e1, e2 = _peer_topk(x, lw)
    return _peer_dense(x, xt, na, r2, e1, e2, lw)


def kernel(x_prompt, x_sample, cache_sb_k, cache_sb_v, state_ml_C, state_ml_n, state_ml_m, state_conv, cache_mem_k, cache_mem_v, page_table, mem_prompt, g_mix, w_in, g_sb_q, g_sb_k, sb_bias, ml_ibias, ml_fbias, g_ml_h, cv_w, cv_b, cv_ln_g, cv_ln_b, w_sb_o, w_ml_o, w_cv_o, w_out, g_xa, g_mem, w_xq, w_xk, w_xv, g_xq, g_xk, w_xo, g_ffn, w_pq, pk_k1, pk_k2, pk_u, pk_v):
    p = dict(g_mix=g_mix, w_in=w_in, g_sb_q=g_sb_q, g_sb_k=g_sb_k, sb_bias=sb_bias, ml_ibias=ml_ibias,
             ml_fbias=ml_fbias, g_ml_h=g_ml_h, cv_w=cv_w, cv_b=cv_b, cv_ln_g=cv_ln_g, cv_ln_b=cv_ln_b,
             w_sb_o=w_sb_o, w_ml_o=w_ml_o, w_cv_o=w_cv_o, w_out=w_out, g_xa=g_xa, g_mem=g_mem, w_xq=w_xq,
             w_xk=w_xk, w_xv=w_xv, g_xq=g_xq, g_xk=g_xk, w_xo=w_xo, g_ffn=g_ffn, w_pq=w_pq,
             pk_k1=pk_k1, pk_k2=pk_k2, pk_u=pk_u, pk_v=pk_v)
    depth = w_in.shape[0]
    B, L, D = x_prompt.shape
    S = x_sample.shape[0]
    nh_sb = sb_bias.shape[1]
    nh_ml = ml_ibias.shape[1]
    sbw = w_sb_o.shape[1]
    M = mem_prompt.shape[1]
    xaw = w_xo.shape[1]
    n_pool, page = cache_sb_k.shape[1], cache_sb_k.shape[2]
    ck = cache_sb_k.transpose(0, 1, 3, 4, 2)
    cv = cache_sb_v.transpose(0, 1, 3, 4, 2)
    mem_flat = mem_prompt.reshape(B * M, D)

    yp, ys = x_prompt, x_sample.reshape(1, S, D)
    outs = [[] for _ in range(14)]
    for l in range(depth):
        lw = _prep_weights(l, p)

        mk = _norm_proj(mem_flat, lw["g_mem"], lw["w_xk"], lw["g_xk"], True, F32).reshape(B, M, xaw)
        mv = _norm_proj(mem_flat, lw["g_mem"], lw["w_xv"], lw["g_xk"], False, F32).reshape(B, M, xaw)

        q, kf, kb, vf, vb, qm, km, vm, og, gcol, grow, u = _in_proj(yp, lw)
        h_sb = _sb_prompt(q, kb, vb, lw["sb_bias"])
        h_ml, Cp, n_p, m_p = _mlstm_prompt(qm, km, vm, og, gcol, grow, lw["g_ml_h"])
        h_cv = _conv_prompt(u, lw["cv_w"], lw["cv_b"], lw["cv_ln_g"], lw["cv_ln_b"])
        xa_p = lambda qq: _xa_prompt(qq.reshape(B, L, xaw), mk, mv).reshape(B * L, xaw)
        yp = _tail(yp.reshape(B * L, D), h_sb.reshape(B * L, -1), h_ml.reshape(B * L, -1),
                   h_cv.reshape(B * L, -1), lw, xa_p).reshape(B, L, D)
        hist_p = u[:, L - (CV_K - 1):, :]
        if L < CV_K - 1:
            hist_p = jnp.concatenate([jnp.zeros((B, CV_K - 1 - L, u.shape[2]), F32), u], axis=1)

        q_s, kf_s, _, vf_s, _, qm_s, km_s, vm_s, og_s, gcol_s, _, u_s = _in_proj(ys, lw)
        h_sb_s = _sb_sample(l, page_table, q_s.astype(F32).reshape(S, nh_sb, SB_HD, 1),
                            lw["sb_bias"].reshape(nh_sb, 1), ck, cv)
        per_seq = lambda a: a.reshape(S, 1, a.shape[-1])
        h_ml_s, Cs, n_s, m_s = _mlstm_sample(
            per_seq(qm_s), per_seq(km_s), per_seq(vm_s), per_seq(og_s), per_seq(gcol_s), lw["g_ml_h"],
            state_ml_C[l], state_ml_n[l].reshape(S, nh_ml, 1, ML_HD), state_ml_m[l].reshape(S, nh_ml, 1, 1))
        xx = jnp.concatenate([state_conv[l], u_s.reshape(S, 1, -1)], axis=1)
        h_cv_s = _conv_sample(xx, lw["cv_w"], lw["cv_b"], lw["cv_ln_g"], lw["cv_ln_b"])
        xa_s = lambda qq: _xa_sample(l, qq.reshape(S, -1, XA_HD), cache_mem_k, cache_mem_v).reshape(S, xaw)
        ys = _tail(ys.reshape(S, D), h_sb_s.reshape(S, -1), h_ml_s.reshape(S, -1), h_cv_s, lw,
                   xa_s).reshape(1, S, D)

        vals = (kf.reshape(B, L, nh_sb, SB_HD), vf.reshape(B, L, nh_sb, SB_HD),
                kf_s.reshape(S, 1, nh_sb, SB_HD), vf_s.reshape(S, 1, nh_sb, SB_HD),
                Cp, n_p.reshape(B, nh_ml, ML_HD), m_p.reshape(B, nh_ml),
                Cs, n_s.reshape(S, nh_ml, ML_HD), m_s.reshape(S, nh_ml),
                hist_p, xx[:, 1:, :],
                mk.reshape(B, M, -1, XA_HD), mv.reshape(B, M, -1, XA_HD))
        for acc, v in zip(outs, vals):
            acc.append(v)
    return (yp, ys.reshape(S, 1, D)) + tuple(jnp.stack(a) for a in outs)
```

```python
import functools

import jax
import jax.numpy as jnp
from jax import lax
from jax.experimental import pallas as pl
from jax.experimental.pallas import tpu as pltpu

F32 = jnp.float32
BF16 = jnp.bfloat16
EPS = 1e-6
NEG_INF = float("-inf")

SB_HD = 64
ML_HD = 128
ML_CHUNK = 128
XA_HD = 128
CV_K = 31
PK_TOPK = 16
PK_NKEYS = 128
LANES = 128
VMEM_LIMIT = 56 * 1024 * 1024


def _params(*sem):
    return pltpu.CompilerParams(dimension_semantics=sem, vmem_limit_bytes=VMEM_LIMIT)


def _dot(a, b):
    return jnp.dot(a, b, preferred_element_type=F32)


def _dot_nt(a, b):
    return lax.dot_general(a, b, (((1,), (1,)), ((), ())), preferred_element_type=F32)


def _split2(x):
    hi = x.astype(BF16)
    lo = (x - hi.astype(F32)).astype(BF16)
    return hi, lo


def _split3(x):
    hi = x.astype(BF16)
    r = x - hi.astype(F32)
    mid = r.astype(BF16)
    lo = (r - mid.astype(F32)).astype(BF16)
    return hi, mid, lo


def _rms(x, g):
    ms = jnp.mean(x * x, axis=-1, keepdims=True)
    return x * lax.rsqrt(ms + EPS) * g


def _log_sigmoid(x):
    return jnp.minimum(x, 0.0) - jnp.log1p(jnp.exp(-jnp.abs(x)))


def _sigmoid(x):
    return 1.0 / (1.0 + jnp.exp(-x))


def _full(shape):
    n = len(shape)
    return pl.BlockSpec(shape, lambda *_: (0,) * n)


def _in_proj_kernel(x_ref, g_ref, wsb_ref, bd_ref, gq_ref, gk_ref, wml_ref,
                    wifh_ref, wifl_ref, bif_ref, wifth_ref, wiftl_ref, bift_ref,
                    wga_ref, wgb_ref,
                    q_ref, kf_ref, kb_ref, vf_ref, vb_ref, qm_ref, km_ref, vm_ref,
                    og_ref, gcol_ref, grow_ref, u_ref):
    x = x_ref[0]
    xn = _rms(x, g_ref[...])
    xh, xl = _split2(xn)

    sbw = q_ref.shape[-1]
    z = _dot(xh, wsb_ref[...])
    bd = bd_ref[...]

    def headnorm(zz, gain):
        sh, sl = _split2(zz * zz)
        ms = (_dot(sh, bd) + _dot(sl, bd)) * (1.0 / SB_HD)
        return zz * lax.rsqrt(ms + EPS) * gain

    qn = headnorm(z[:, :sbw], gq_ref[...])
    kn = headnorm(z[:, sbw:2 * sbw], gk_ref[...])
    vv = z[:, 2 * sbw:]
    q_ref[0] = (qn * (SB_HD ** -0.5)).astype(BF16)
    kf_ref[0] = kn
    kb_ref[0] = kn.astype(BF16)
    vf_ref[0] = vv
    vb_ref[0] = vv.astype(BF16)

    mlw = qm_ref.shape[-1]
    zm = _dot(xh, wml_ref[...])
    qm_ref[0] = zm[:, :mlw].astype(BF16)
    km_ref[0] = (zm[:, mlw:2 * mlw] * (ML_HD ** -0.5)).astype(BF16)
    vm_ref[0] = zm[:, 2 * mlw:3 * mlw].astype(BF16)
    og_ref[0] = _sigmoid(zm[:, 3 * mlw:])

    nh = grow_ref.shape[1] // 2
    wh, wl = wifh_ref[...], wifl_ref[...]
    zc = _dot(xh, wh) + _dot(xh, wl) + _dot(xl, wh) + bif_ref[...]
    lane = lax.broadcasted_iota(jnp.int32, zc.shape, 1)
    gcol_ref[0] = jnp.where(lane >= nh, _log_sigmoid(zc), zc)
    wth, wtl = wifth_ref[...], wiftl_ref[...]
    zr = _dot_nt(wth, xh) + _dot_nt(wtl, xh) + _dot_nt(wth, xl) + bift_ref[...]
    row = lax.broadcasted_iota(jnp.int32, zr.shape, 0)
    grow_ref[0] = jnp.where(row >= nh, _log_sigmoid(zr), zr)

    u_ref[0] = _dot(xh, wga_ref[...]) * _sigmoid(_dot(xh, wgb_ref[...]))


def _in_proj(x, lw):
    B, L, D = x.shape
    tm = min(L, 256)
    sbw = lw["w_sb"].shape[1] // 3
    mlw = lw["w_ml"].shape[1] // 4
    cvw = lw["w_ga"].shape[1]
    nh2 = lw["b_ift"].shape[0]
    tok = lambda c: pl.BlockSpec((1, tm, c), lambda b, i: (b, i, 0))
    weights = [lw["g_mix"], lw["w_sb"], lw["bd64"], lw["g_sb_q"], lw["g_sb_k"], lw["w_ml"],
               lw["w_if_h"], lw["w_if_l"], lw["b_if"], lw["w_ift_h"], lw["w_ift_l"], lw["b_ift"],
               lw["w_ga"], lw["w_gb"]]
    out_shape = [
        jax.ShapeDtypeStruct((B, L, sbw), BF16),
        jax.ShapeDtypeStruct((B, L, sbw), F32),
        jax.ShapeDtypeStruct((B, L, sbw), BF16),
        jax.ShapeDtypeStruct((B, L, sbw), F32),
        jax.ShapeDtypeStruct((B, L, sbw), BF16),
        jax.ShapeDtypeStruct((B, L, mlw), BF16),
        jax.ShapeDtypeStruct((B, L, mlw), BF16),
        jax.ShapeDtypeStruct((B, L, mlw), BF16),
        jax.ShapeDtypeStruct((B, L, mlw), F32),
        jax.ShapeDtypeStruct((B, L, LANES), F32),
        jax.ShapeDtypeStruct((B, nh2, L), F32),
        jax.ShapeDtypeStruct((B, L, cvw), F32),
    ]
    out_specs = [tok(sbw)] * 5 + [tok(mlw)] * 4 + [tok(LANES),
                 pl.BlockSpec((1, nh2, tm), lambda b, i: (b, 0, i)), tok(cvw)]
    return pl.pallas_call(
        _in_proj_kernel,
        grid=(B, L // tm),
        in_specs=[tok(D)] + [_full(w.shape) for w in weights],
        out_specs=out_specs,
        out_shape=out_shape,
        compiler_params=_params("parallel", "parallel"),
        name="in_proj",
    )(x, *weights)


def _split_trunc(x):
    hi = lax.bitcast_convert_type(lax.bitcast_convert_type(x, jnp.uint32) & jnp.uint32(0xFFFF0000), F32)
    return hi.astype(BF16), (x - hi).astype(BF16)


def _sb_logs(z, mask):
    l = jnp.log(1.0 + jnp.exp(-jnp.abs(z)))
    lb = jnp.minimum(z, 0.0) - l
    l1 = lb - z
    if mask is not None:
        l1 = jnp.where(mask, l1, 0.0)
    return lb, l1


SB_GROUP = 8


def _sb_prompt_kernel(bias_ref, q_ref, k_ref, v_ref, o_ref, *, tq):
    hg = pl.program_id(1)
    i = pl.program_id(2)
    pair = 2 * SB_HD
    lane = lax.broadcasted_iota(jnp.int32, (tq, pair), 1)
    qs, biases = [], []
    for h in range(SB_GROUP):
        qp = q_ref[0, :, (h // 2) * pair:(h // 2 + 1) * pair]
        keep = (lane < SB_HD) if h % 2 == 0 else (lane >= SB_HD)
        qs.append(jnp.where(keep, qp, jnp.zeros_like(qp)))
        biases.append(bias_ref[SB_GROUP * hg + h])
    row = lax.broadcasted_iota(jnp.int32, (tq, tq), 0)
    col = lax.broadcasted_iota(jnp.int32, (tq, tq), 1)
    tri = (row > col).astype(BF16)
    causal = col < row

    def step(j, carry, mask):
        start = pl.multiple_of(j * tq, tq)
        heads = range(SB_GROUP)
        cols = [slice((h // 2) * pair, (h // 2 + 1) * pair) for h in heads]
        z = [_dot_nt(qs[h], k_ref[0, pl.ds(start, tq), cols[h]]) + biases[h] for h in heads]
        logs = [_sb_logs(z[h], mask) for h in heads]
        parts = [_split_trunc(logs[h][1]) for h in heads]
        within = [_dot(parts[h][0], tri) + _dot(parts[h][1], tri) for h in heads]
        w = []
        for h in heads:
            wh = jnp.exp(logs[h][0] + within[h] + carry[h][1])
            w.append((jnp.where(mask, wh, 0.0) if mask is not None else wh).astype(BF16))
        out = []
        for h in heads:
            pv = _dot(w[h], v_ref[0, pl.ds(start, tq), cols[h]])
            c = carry[h][1] + within[h][:, 0:1] + logs[h][1][:, 0:1]
            out.append((carry[h][0] + pv, c))
        return tuple(out)

    init = tuple((jnp.zeros((tq, pair), F32), jnp.zeros((tq, 1), F32)) for _ in range(SB_GROUP))
    carry = step(i, init, causal)
    carry = lax.fori_loop(0, i, lambda t, cr: step(i - 1 - t, cr, None), carry)
    for p in range(SB_GROUP // 2):
        o_ref[0, :, p * pair:(p + 1) * pair] = jnp.where(
            lane < SB_HD, carry[2 * p][0], carry[2 * p + 1][0]).astype(o_ref.dtype)


def _sb_prompt(q, k, v, bias):
    B, L, W = q.shape
    tq = min(L, 256)
    gw = SB_GROUP * SB_HD
    return pl.pallas_call(
        functools.partial(_sb_prompt_kernel, tq=tq),
        grid_spec=pltpu.PrefetchScalarGridSpec(
            num_scalar_prefetch=1,
            grid=(B, W // gw, L // tq),
            in_specs=[pl.BlockSpec((1, tq, gw), lambda b, h, i, *_: (b, i, h)),
                      pl.BlockSpec((1, L, gw), lambda b, h, i, *_: (b, 0, h)),
                      pl.BlockSpec((1, L, gw), lambda b, h, i, *_: (b, 0, h))],
            out_specs=pl.BlockSpec((1, tq, gw), lambda b, h, i, *_: (b, i, h)),
        ),
        out_shape=jax.ShapeDtypeStruct((B, L, W), BF16),
        compiler_params=_params("parallel", "parallel", "parallel"),
        name="sb_prompt",
    )(bias, q, k, v)


SB_PAGES_PER_STEP = 8


def _bf16_round(x):
    return x.astype(BF16).astype(F32)


def _sb_sample_kernel(pt_ref, q_ref, bias_ref, *refs, pps):
    k_refs = refs[:pps]
    v_refs = refs[pps:2 * pps]
    o_ref, acc_ref, c_ref = refs[2 * pps:]
    g = pl.program_id(1)
    nh, hd, page = k_refs[0].shape
    bias = bias_ref[...]

    @pl.when(g == 0)
    def _():
        acc_ref[...] = jnp.zeros_like(acc_ref)
        c_ref[...] = jnp.zeros_like(c_ref)

    qb = q_ref[0] + jnp.zeros((nh, hd, page), F32)
    lane = lax.broadcasted_iota(jnp.int32, (nh, page), 1)
    pages = range(pps)
    z = [jnp.sum(k_refs[t][...] * qb, axis=1) + bias for t in pages]
    logs = [_sb_logs(z[t], None) for t in pages]
    incl = []
    for t in pages:
        x = logs[t][1]
        d = 1
        while d < page:
            x = x + jnp.where(lane < page - d, pltpu.roll(x, page - d, axis=1), 0.0)
            d *= 2
        incl.append(x)
    c = c_ref[...]
    acc = acc_ref[...]
    for t in pages:
        tail = incl[t] - logs[t][1] + c
        w = _bf16_round(jnp.exp(logs[t][0] + tail))
        acc = acc + w[:, None, :] * v_refs[t][...]
        c = c + incl[t][:, 0:1]
    acc_ref[...] = acc
    c_ref[...] = c

    @pl.when(g == pl.num_programs(1) - 1)
    def _():
        ones = jnp.ones((8, page), BF16)
        o = sum(_dot_nt(ones, part) for part in _split3(acc.reshape(nh * hd, page)))
        o_ref[0] = o[0:1].astype(o_ref.dtype)


def _sb_sample(layer, page_table, q, bias_col, cache_k, cache_v):
    S, n_pages = page_table.shape
    _, _, nh, hd, page = cache_k.shape
    pps = min(SB_PAGES_PER_STEP, n_pages)

    def page_spec(t):
        return pl.BlockSpec((None, None, nh, hd, page),
                            lambda b, g, pt: (layer, pt[b, n_pages - 1 - (g * pps + t)], 0, 0, 0))

    return pl.pallas_call(
        functools.partial(_sb_sample_kernel, pps=pps),
        grid_spec=pltpu.PrefetchScalarGridSpec(
            num_scalar_prefetch=1,
            grid=(S, n_pages // pps),
            in_specs=[pl.BlockSpec((1, nh, hd, 1), lambda b, g, pt: (b, 0, 0, 0)),
                      pl.BlockSpec((nh, 1), lambda b, g, pt: (0, 0))]
                     + [page_spec(t) for t in range(pps)] * 2,
            out_specs=pl.BlockSpec((1, 1, nh * hd), lambda b, g, pt: (b, 0, 0)),
            scratch_shapes=[pltpu.VMEM((nh, hd, page), F32), pltpu.VMEM((nh, 1), F32)],
        ),
        out_shape=jax.ShapeDtypeStruct((S, 1, nh * hd), BF16),
        compiler_params=_params("parallel", "arbitrary"),
        name="sb_sample",
    )(page_table, q, bias_col, *([cache_k] * pps), *([cache_v] * pps))


def _mlstm_chunk_math(q, k, v, ig_col, ig_row, bcum_col, bcum_row, C, n, m):
    L = q.shape[0]
    row = lax.broadcasted_iota(jnp.int32, (L, L), 0)
    col = lax.broadcasted_iota(jnp.int32, (L, L), 1)
    log_d = jnp.where(col <= row, bcum_col - bcum_row + ig_row, NEG_INF)
    log_c = bcum_col + m
    m_t = jnp.maximum(log_c, jnp.max(log_d, axis=-1, keepdims=True))
    w_d = jnp.exp(log_d - m_t)
    w_c = jnp.exp(log_c - m_t)
    s = _dot_nt(q, k) * w_d
    num = _dot(s.astype(BF16), v) + w_c * _dot(q, C.astype(BF16))
    qf = q.astype(F32)
    den = jnp.sum(s, axis=-1, keepdims=True) + w_c * jnp.sum(qf * n, axis=-1, keepdims=True)
    h = num / jnp.maximum(jnp.abs(den), jnp.exp(-m_t))
    b_end = bcum_col[L - 1:L, :]
    log_in = b_end - bcum_col + ig_col
    m_new = jnp.maximum(b_end + m, jnp.max(log_in, axis=0, keepdims=True))
    w_in = jnp.exp(log_in - m_new)
    decay = jnp.exp(b_end + m - m_new)
    kw = k.astype(F32) * w_in
    C_new = decay * C + _dot(kw.T.astype(BF16), v)
    n_new = decay * n + jnp.sum(kw, axis=0, keepdims=True)
    return h, C_new, n_new, m_new


def _mlstm_prompt_kernel(q_ref, k_ref, v_ref, og_ref, gcol_ref, grow_ref, gh_ref,
                         h_ref, c_out, n_out, m_out, c_scr, n_scr, m_scr):
    t = pl.program_id(1)
    nh = c_scr.shape[0]
    L = q_ref.shape[1]

    @pl.when(t == 0)
    def _():
        c_scr[...] = jnp.zeros_like(c_scr)
        n_scr[...] = jnp.zeros_like(n_scr)
        m_scr[...] = jnp.zeros_like(m_scr)

    row = lax.broadcasted_iota(jnp.int32, (L, L), 0)
    col = lax.broadcasted_iota(jnp.int32, (L, L), 1)
    lower = (col <= row).astype(BF16)
    upper = (row <= col).astype(BF16)
    gcol = gcol_ref[0]
    grow = grow_ref[0]
    cum_col = sum(_dot(lower, part) for part in _split3(gcol))
    cum_row = sum(_dot(part, upper) for part in _split3(grow))
    for h in range(nh):
        sl = slice(h * ML_HD, (h + 1) * ML_HD)
        q, k, v = q_ref[0, :, sl], k_ref[0, :, sl], v_ref[0, :, sl]
        hh, C, n, m = _mlstm_chunk_math(
            q, k, v, gcol[:, h:h + 1], grow[h:h + 1, :],
            cum_col[:, nh + h:nh + h + 1], cum_row[nh + h:nh + h + 1, :],
            c_scr[h], n_scr[h], m_scr[h])
        c_scr[h] = C
        n_scr[h] = n
        m_scr[h] = m
        hn = _rms(hh, gh_ref[h:h + 1, :])
        h_ref[0, :, sl] = (og_ref[0, :, sl] * hn).astype(h_ref.dtype)

    @pl.when(t == pl.num_programs(1) - 1)
    def _():
        c_out[0] = c_scr[...]
        n_out[0] = n_scr[...]
        m_out[0] = m_scr[...]


def _mlstm_prompt(qm, km, vm, og, gcol, grow, g_h):
    B, L, W = qm.shape
    nh = W // ML_HD
    ch = min(L, ML_CHUNK)
    tok = lambda c: pl.BlockSpec((1, ch, c), lambda b, t: (b, t, 0))
    return pl.pallas_call(
        _mlstm_prompt_kernel,
        grid=(B, L // ch),
        in_specs=[tok(W), tok(W), tok(W), tok(W), tok(gcol.shape[-1]),
                  pl.BlockSpec((1, 2 * nh, ch), lambda b, t: (b, 0, t)),
                  _full(g_h.shape)],
        out_specs=[tok(W),
                   pl.BlockSpec((1, nh, ML_HD, ML_HD), lambda b, t: (b, 0, 0, 0)),
                   pl.BlockSpec((1, nh, 1, ML_HD), lambda b, t: (b, 0, 0, 0)),
                   pl.BlockSpec((1, nh, 1, 1), lambda b, t: (b, 0, 0, 0))],
        out_shape=[jax.ShapeDtypeStruct((B, L, W), BF16),
                   jax.ShapeDtypeStruct((B, nh, ML_HD, ML_HD), F32),
                   jax.ShapeDtypeStruct((B, nh, 1, ML_HD), F32),
                   jax.ShapeDtypeStruct((B, nh, 1, 1), F32)],
        scratch_shapes=[pltpu.VMEM((nh, ML_HD, ML_HD), F32),
                        pltpu.VMEM((nh, 1, ML_HD), F32),
                        pltpu.VMEM((nh, 1, 1), F32)],
        compiler_params=_params("parallel", "arbitrary"),
        name="mlstm_prompt",
    )(qm, km, vm, og, gcol, grow, g_h)


def _mlstm_sample_kernel(q_ref, k_ref, v_ref, og_ref, g_ref, gh_ref, c_ref, n_ref, m_ref,
                         h_ref, c_out, n_out, m_out):
    nh = c_ref.shape[1]
    d = c_ref.shape[-1]
    eye = (lax.broadcasted_iota(jnp.int32, (d, d), 0) == lax.broadcasted_iota(jnp.int32, (d, d), 1))

    def to_col(r):
        return jnp.sum(jnp.where(eye, r, 0.0), axis=1, keepdims=True)

    g = g_ref[0]
    for h in range(nh):
        sl = slice(h * d, (h + 1) * d)
        q = q_ref[0, :, sl].astype(F32)
        k = k_ref[0, :, sl].astype(F32)
        v = v_ref[0, :, sl].astype(F32)
        C = c_ref[0, h]
        n = n_ref[0, h]
        m = m_ref[0, h]
        ig = g[:, h:h + 1]
        lf = g[:, nh + h:nh + h + 1]
        log_c = lf + m
        m_t = jnp.maximum(log_c, ig)
        w_d = jnp.exp(ig - m_t)
        w_c = jnp.exp(log_c - m_t)
        s = jnp.sum(q * k, axis=-1, keepdims=True) * w_d
        cb = C.astype(BF16).astype(F32)
        qC = jnp.sum(to_col(q) * cb, axis=0, keepdims=True)
        num = s * v + w_c * qC
        den = s + w_c * jnp.sum(q * n, axis=-1, keepdims=True)
        hh = num / jnp.maximum(jnp.abs(den), jnp.exp(-m_t))
        c_out[0, h] = w_c * C + w_d * (to_col(k) * v)
        n_out[0, h] = w_c * n + w_d * k
        m_out[0, h] = m_t
        hn = _rms(hh, gh_ref[h:h + 1, :])
        h_ref[0, :, sl] = (og_ref[0, :, sl] * hn).astype(h_ref.dtype)


def _mlstm_sample(qm, km, vm, og, gcol, g_h, C, n, m):
    S, _, W = qm.shape
    nh = C.shape[1]
    d = C.shape[-1]
    seq = lambda *tail: pl.BlockSpec((1,) + tail, lambda b: (b,) + (0,) * len(tail))
    return pl.pallas_call(
        _mlstm_sample_kernel,
        grid=(S,),
        in_specs=[seq(1, W), seq(1, W), seq(1, W), seq(1, W), seq(1, gcol.shape[-1]), _full(g_h.shape),
                  seq(nh, d, d), seq(nh, 1, d), seq(nh, 1, 1)],
        out_specs=[seq(1, W), seq(nh, d, d), seq(nh, 1, d), seq(nh, 1, 1)],
        out_shape=[jax.ShapeDtypeStruct((S, 1, W), BF16),
                   jax.ShapeDtypeStruct(C.shape, F32),
                   jax.ShapeDtypeStruct(n.shape, F32),
                   jax.ShapeDtypeStruct(m.shape, F32)],
        compiler_params=_params("parallel"),
        name="mlstm_sample",
    )(qm, km, vm, og, gcol, g_h, C, n, m)


def _ln_swish(c, g, b):
    cc = c - jnp.mean(c, axis=-1, keepdims=True)
    y = cc * lax.rsqrt(jnp.mean(cc * cc, axis=-1, keepdims=True) + EPS) * g + b
    return y * _sigmoid(y)


CV_HALO = 32


def _conv_prompt_kernel(u_ref, halo_ref, w_ref, b_ref, g_ref, beta_ref, o_ref, win_ref):
    i = pl.program_id(1)
    tl = u_ref.shape[1]
    halo = halo_ref[0]
    win_ref[0:CV_HALO, :] = jnp.where(i == 0, jnp.zeros_like(halo), halo)
    win_ref[CV_HALO:, :] = u_ref[0]
    off = CV_HALO - (CV_K - 1)
    acc = jnp.zeros((tl, u_ref.shape[2]), F32) + b_ref[...]
    for j in range(CV_K):
        acc = acc + w_ref[j:j + 1, :] * win_ref[off + j:off + j + tl, :]
    o_ref[0] = _ln_swish(acc, g_ref[...], beta_ref[...]).astype(o_ref.dtype)


def _conv_prompt(u, w, b, g, beta):
    B, L, C = u.shape
    tl = min(L, 256)
    r = tl // CV_HALO
    return pl.pallas_call(
        _conv_prompt_kernel,
        grid=(B, L // tl),
        in_specs=[pl.BlockSpec((1, tl, C), lambda b_, i: (b_, i, 0)),
                  pl.BlockSpec((1, CV_HALO, C), lambda b_, i: (b_, jnp.maximum(i * r - 1, 0), 0)),
                  _full(w.shape), _full(b.shape), _full(g.shape), _full(beta.shape)],
        out_specs=pl.BlockSpec((1, tl, C), lambda b_, i: (b_, i, 0)),
        out_shape=jax.ShapeDtypeStruct((B, L, C), BF16),
        scratch_shapes=[pltpu.VMEM((tl + CV_HALO, C), F32)],
        compiler_params=_params("parallel", "parallel"),
        name="conv_prompt",
    )(u, u, w, b, g, beta)


def _conv_sample_kernel(xx_ref, w_ref, b_ref, g_ref, beta_ref, o_ref):
    xx = xx_ref[...]
    c = jnp.sum(xx * w_ref[...][None], axis=1) + b_ref[...]
    o_ref[...] = _ln_swish(c, g_ref[...], beta_ref[...]).astype(o_ref.dtype)


def _conv_sample(xx, w, b, g, beta):
    S, K, C = xx.shape
    nb = min(S, 8)
    return pl.pallas_call(
        _conv_sample_kernel,
        grid=(S // nb,),
        in_specs=[pl.BlockSpec((nb, K, C), lambda s: (s, 0, 0)),
                  _full(w.shape), _full(b.shape), _full(g.shape), _full(beta.shape)],
        out_specs=pl.BlockSpec((nb, C), lambda s: (s, 0)),
        out_shape=jax.ShapeDtypeStruct((S, C), BF16),
        compiler_params=_params("parallel"),
        name="conv_sample",
    )(xx, w, b, g, beta)


def _merge_kernel(x_ref, g_ref, wg_ref, hs_ref, hm_ref, hc_ref, ws_ref, wm_ref, wc_ref, wo_ref, o_ref):
    x = x_ref[...]
    D = x.shape[-1]
    xn = _rms(x, g_ref[...]).astype(BF16)
    merged = None
    for b, (h_ref, w_ref) in enumerate(((hs_ref, ws_ref), (hm_ref, wm_ref), (hc_ref, wc_ref))):
        gate = _sigmoid(_dot(xn, wg_ref[:, b * D:(b + 1) * D]))
        term = gate * _dot(h_ref[...], w_ref[...])
        merged = term if merged is None else merged + term
    o_ref[...] = x + _dot(merged.astype(BF16), wo_ref[...])


def _merge(x, h_sb, h_ml, h_cv, lw):
    T, D = x.shape
    tm = min(T, 256)
    W = h_sb.shape[1]
    tok = lambda c: pl.BlockSpec((tm, c), lambda i: (i, 0))
    weights = [lw["w_sb_o"], lw["w_ml_o"], lw["w_cv_o"], lw["w_out"]]
    return pl.pallas_call(
        _merge_kernel,
        grid=(T // tm,),
        in_specs=[tok(D), _full(lw["g_mix"].shape), _full(lw["w_gates"].shape), tok(W), tok(W), tok(W)]
                 + [_full(w.shape) for w in weights],
        out_specs=tok(D),
        out_shape=jax.ShapeDtypeStruct((T, D), F32),
        compiler_params=_params("parallel"),
        name="merge",
    )(x, lw["g_mix"], lw["w_gates"], h_sb, h_ml, h_cv, *weights)


def _norm_proj_kernel(x_ref, g_ref, w_ref, gh_ref, o_ref, *, head_norm):
    xn = _rms(x_ref[...], g_ref[...]).astype(BF16)
    z = _dot(xn, w_ref[...])
    if head_norm:
        for h in range(z.shape[1] // XA_HD):
            sl = slice(h * XA_HD, (h + 1) * XA_HD)
            o_ref[:, sl] = _rms(z[:, sl], gh_ref[...]).astype(o_ref.dtype)
    else:
        o_ref[...] = z.astype(o_ref.dtype)


def _norm_proj(x, g, w, gh, head_norm, out_dtype):
    T, D = x.shape
    N = w.shape[1]
    tm = min(T, 256)
    return pl.pallas_call(
        functools.partial(_norm_proj_kernel, head_norm=head_norm),
        grid=(T // tm,),
        in_specs=[pl.BlockSpec((tm, D), lambda i: (i, 0)), _full(g.shape), _full(w.shape), _full(gh.shape)],
        out_specs=pl.BlockSpec((tm, N), lambda i: (i, 0)),
        out_shape=jax.ShapeDtypeStruct((T, N), out_dtype),
        compiler_params=_params("parallel"),
        name="norm_proj",
    )(x, g, w, gh)


def _proj_res_kernel(x_ref, h_ref, w_ref, o_ref):
    o_ref[...] = x_ref[...] + _dot(h_ref[...], w_ref[...])


def _proj_res(x, h, w):
    T, D = x.shape
    tm = min(T, 256)
    return pl.pallas_call(
        _proj_res_kernel,
        grid=(T // tm,),
        in_specs=[pl.BlockSpec((tm, D), lambda i: (i, 0)),
                  pl.BlockSpec((tm, h.shape[1]), lambda i: (i, 0)), _full(w.shape)],
        out_specs=pl.BlockSpec((tm, D), lambda i: (i, 0)),
        out_shape=jax.ShapeDtypeStruct((T, D), F32),
        compiler_params=_params("parallel"),
        name="proj_res",
    )(x, h, w)


def _xa_prompt_kernel(q_ref, k_ref, v_ref, o_ref):
    k = k_ref[0].astype(BF16)
    v = v_ref[0].astype(BF16)
    for h in range(q_ref.shape[-1] // XA_HD):
        sl = slice(h * XA_HD, (h + 1) * XA_HD)
        s = _dot_nt(q_ref[0, :, sl], k[:, sl]) * (XA_HD ** -0.5)
        e = jnp.exp(s - jnp.max(s, axis=-1, keepdims=True))
        p = e / jnp.sum(e, axis=-1, keepdims=True)
        o_ref[0, :, sl] = _dot(p.astype(BF16), v[:, sl]).astype(o_ref.dtype)


def _xa_prompt(q, mk, mv):
    B, L, W = q.shape
    M = mk.shape[1]
    tm = min(L, 512)
    return pl.pallas_call(
        _xa_prompt_kernel,
        grid=(B, L // tm),
        in_specs=[pl.BlockSpec((1, tm, W), lambda b, i: (b, i, 0)),
                  pl.BlockSpec((1, M, W), lambda b, i: (b, 0, 0)),
                  pl.BlockSpec((1, M, W), lambda b, i: (b, 0, 0))],
        out_specs=pl.BlockSpec((1, tm, W), lambda b, i: (b, i, 0)),
        out_shape=jax.ShapeDtypeStruct((B, L, W), BF16),
        compiler_params=_params("parallel", "parallel"),
        name="xa_prompt",
    )(q, mk, mv)


def _xa_sample_kernel(q_ref, k_ref, v_ref, o_ref):
    q = q_ref[0].astype(F32)
    k = k_ref[...].astype(BF16).astype(F32)
    v = v_ref[...].astype(BF16).astype(F32)
    s = jnp.sum(k * q[None], axis=-1, keepdims=True) * (XA_HD ** -0.5)
    e = jnp.exp(s - jnp.max(s, axis=0, keepdims=True))
    p = e / jnp.sum(e, axis=0, keepdims=True)
    p = p.astype(BF16).astype(F32)
    o_ref[0] = jnp.sum(p * v, axis=0).astype(o_ref.dtype)


def _xa_sample(layer, q, cache_k, cache_v):
    S, nh, hd = q.shape
    M = cache_k.shape[2]
    mem = pl.BlockSpec((None, None, M, nh, hd), lambda b: (layer, b, 0, 0, 0))
    return pl.pallas_call(
        _xa_sample_kernel,
        grid=(S,),
        in_specs=[pl.BlockSpec((1, nh, hd), lambda b: (b, 0, 0)), mem, mem],
        out_specs=pl.BlockSpec((1, nh, hd), lambda b: (b, 0, 0)),
        out_shape=jax.ShapeDtypeStruct((S, nh, hd), BF16),
        compiler_params=_params("parallel"),
        name="xa_sample",
    )(q, cache_k, cache_v)


NOT_TAKEN = 99.0


def _take_top(s, n, exact):
    idx = lax.broadcasted_iota(jnp.int32, s.shape, 0)
    big = jnp.int32(s.shape[0])
    slot = lax.broadcasted_iota(jnp.int32, (n, s.shape[1]), 0)
    vals = jnp.zeros((n, s.shape[1]), F32)
    rank = jnp.full(s.shape, NOT_TAKEN, F32)
    for r in range(n):
        mx = jnp.max(s, axis=0, keepdims=True)
        if exact:
            first = jnp.min(jnp.where(s == mx, idx, big), axis=0, keepdims=True)
            hit = idx == first
        else:
            hit = s == mx
        s = jnp.where(hit, NEG_INF, s)
        rank = jnp.where(hit, float(r), rank)
        vals = jnp.where(slot == r, mx, vals)
    excess = jnp.sum(jnp.where(rank < NOT_TAKEN, 1.0, 0.0), axis=0, keepdims=True) - float(n)
    return vals, rank, excess


def _cand_pieces(v1, v2):
    sub = lax.broadcasted_iota(jnp.int32, (8, v1.shape[1]), 0)
    pieces = [v1[0:1] + v2]
    for i in range(1, 8):
        nj = PK_TOPK // (i + 1)
        pieces.append(jnp.where(sub < nj, v1[i:i + 1] + v2[0:8], NEG_INF))
    pieces.append(v1[8:16] + v2[0:1])
    return jnp.concatenate(pieces, axis=0)


def _peer_select(x_ref, g_ref, wq_ref, k1h_ref, k1l_ref, k2h_ref, k2l_ref,
                 xt_ref, na_ref, r2_ref, e1_ref, e2_ref, exact):
    xn = _rms(x_ref[...], g_ref[...])
    xt_ref[...] = xn.T.astype(BF16)
    q = _dot(xn.astype(BF16), wq_ref[...])
    nheads = k1h_ref.shape[0]
    half = k1h_ref.shape[2]
    ties = jnp.zeros((1, q.shape[0]), F32)
    for h in range(nheads):
        q1h, q1l = _split2(q[:, (2 * h) * half:(2 * h + 1) * half])
        q2h, q2l = _split2(q[:, (2 * h + 1) * half:(2 * h + 2) * half])
        s1 = _dot_nt(k1h_ref[h], q1h) + _dot_nt(k1l_ref[h], q1h) + _dot_nt(k1h_ref[h], q1l)
        s2 = _dot_nt(k2h_ref[h], q2h) + _dot_nt(k2l_ref[h], q2h) + _dot_nt(k2h_ref[h], q2l)
        v1, r1, x1 = _take_top(s1, PK_TOPK, exact)
        v2, r2, x2 = _take_top(s2, PK_TOPK, exact)
        top, rc, xc = _take_top(_cand_pieces(v1, v2), PK_TOPK, exact)
        ties = jnp.maximum(ties, jnp.maximum(jnp.maximum(x1, x2), xc))
        zsum = jnp.sum(jnp.exp(top - top[0:1]), axis=0, keepdims=True)
        sel = jnp.where(rc < NOT_TAKEN, 1.0, 0.0)
        na = jnp.zeros_like(s1)
        for i in range(PK_TOPK):
            if i == 0:
                ni = jnp.sum(sel[0:PK_TOPK], axis=0, keepdims=True)
            elif i < 8:
                ni = jnp.sum(sel[PK_TOPK + 8 * (i - 1):PK_TOPK + 8 * i], axis=0, keepdims=True)
            else:
                ni = sel[PK_TOPK + 48 + i:PK_TOPK + 49 + i]
            na = jnp.where(r1 == float(i), ni, na)
        na_ref[h] = na
        r2_ref[h] = r2
        e1_ref[h] = jnp.where(r1 < NOT_TAKEN, jnp.exp(s1 - v1[0:1]) / zsum, 0.0)
        e2_ref[h] = jnp.where(r2 < NOT_TAKEN, jnp.exp(s2 - v2[0:1]), 0.0)
    return jnp.max(ties)


def _peer_topk_kernel(*refs):
    ties = _peer_select(*refs, exact=False)

    @pl.when(ties > 0.0)
    def _():
        _peer_select(*refs, exact=True)


def _peer_topk(x, lw):
    T, D = x.shape
    tm = min(T, 256)
    nheads, nkeys, _ = lw["pk_k1_h"].shape
    weights = [lw["g_ffn"], lw["w_pq"], lw["pk_k1_h"], lw["pk_k1_l"], lw["pk_k2_h"], lw["pk_k2_l"]]
    key_arr = lambda dt: jax.ShapeDtypeStruct((nheads, nkeys, T), dt)
    key_spec = pl.BlockSpec((nheads, nkeys, tm), lambda i: (0, 0, i))
    return pl.pallas_call(
        _peer_topk_kernel,
        grid=(T // tm,),
        in_specs=[pl.BlockSpec((tm, D), lambda i: (i, 0))] + [_full(w.shape) for w in weights],
        out_specs=[pl.BlockSpec((D, tm), lambda i: (0, i))] + [key_spec] * 4,
        out_shape=[jax.ShapeDtypeStruct((D, T), BF16)] + [key_arr(F32)] * 4,
        compiler_params=_params("parallel"),
        name="peer_topk",
    )(x, *weights)


def _gelu(x):
    return 0.5 * x * (1.0 + lax.erf(x * (2.0 ** -0.5)))


PEER_ROWS = 32
PEER_AGROUP = 4
PEER_MM_ROWS = 1024


def _peer_dense_kernel(x_ref, xt_ref, u_ref, vt_ref, na_ref, r2_ref, e1_ref, e2_ref,
                       o_ref, acc_ref, act_ref, gate_ref, w_ref):
    j = pl.program_id(1)
    last = pl.num_programs(1) - 1
    nheads, nkeys, tm = r2_ref.shape
    na = na_ref.shape[1]
    slot = j % 2

    @pl.when(j == 0)
    def _():
        acc_ref[...] = jnp.zeros_like(acc_ref)
        w_ref[1] = jnp.zeros(w_ref.shape[1:], w_ref.dtype)

    cw = min(tm, LANES)
    mw = min(tm, 2 * LANES)
    sub = 8
    blk = (PEER_ROWS // sub, sub, cw)

    def apply_values(rows, cols):
        acc_ref[rows, cols] += _dot(vt_ref[rows, :], w_ref[1 - slot, :, cols])

    def activations(rows, cols):
        act_ref[rows, cols] = _dot(u_ref[rows, :], xt_ref[:, cols])

    def gate_piece(c, bq, ag):
        ln = slice(c * cw, (c + 1) * cw)
        rb = slice(bq * PEER_ROWS, (bq + 1) * PEER_ROWS)
        g = [jnp.zeros(blk, F32) for _ in range(PEER_AGROUP)]
        for h in range(nheads):
            r2 = r2_ref[h, rb, ln].reshape(blk)
            e2 = e2_ref[h, rb, ln].reshape(blk)
            for k in range(PEER_AGROUP):
                n_a = jnp.broadcast_to(na_ref[h, ag + k:ag + k + 1, ln], (sub, cw))[None]
                e1 = jnp.broadcast_to(e1_ref[h, ag + k:ag + k + 1, ln], (sub, cw))[None]
                g[k] = g[k] + jnp.where(r2 < n_a, e1 * e2, 0.0)
        for k in range(PEER_AGROUP):
            r0 = (ag + k) * nkeys + bq * PEER_ROWS
            gate_ref[r0:r0 + PEER_ROWS, ln] = g[k].reshape(PEER_ROWS, cw)

    te, D = u_ref.shape
    mm = []
    for m in range(tm // mw):
        cols = slice(m * mw, (m + 1) * mw)
        for r in range(0, D, PEER_MM_ROWS):
            mm.append(functools.partial(apply_values, slice(r, r + PEER_MM_ROWS), cols))
        for r in range(0, te, PEER_MM_ROWS):
            mm.append(functools.partial(activations, slice(r, r + PEER_MM_ROWS), cols))
    gates = [functools.partial(gate_piece, c, bq, ag) for c in range(tm // cw)
             for bq in range(nkeys // PEER_ROWS) for ag in range(0, na, PEER_AGROUP)]
    per = -(-len(gates) // len(mm))

    @pl.when(j < last)
    def _():
        for i, piece in enumerate(mm):
            piece()
            for gp in gates[i * per:(i + 1) * per]:
                gp()
        for gp in gates[len(mm) * per:]:
            gp()
        w_ref[slot] = (gate_ref[...] * _gelu(act_ref[...])).astype(BF16)

    @pl.when(j == last)
    def _():
        acc = acc_ref[...] + _dot(vt_ref[...], w_ref[1 - slot])
        o_ref[...] = x_ref[...] + acc.T


def _peer_dense(x, xt, na_, r2, e1, e2, lw):
    T, D = x.shape
    u, vt = lw["pk_u"], lw["pk_vt"]
    E = u.shape[0]
    nheads, nkeys, _ = r2.shape
    tm = min(T, 512)
    na = 8
    te = na * nkeys
    nt = E // te
    cur = lambda j: jnp.minimum(j, nt - 1)
    key_spec = pl.BlockSpec((nheads, nkeys, tm), lambda i, j: (0, 0, i))
    a_spec = pl.BlockSpec((nheads, na, tm), lambda i, j: (0, cur(j), i))
    return pl.pallas_call(
        _peer_dense_kernel,
        grid=(T // tm, nt + 1),
        in_specs=[pl.BlockSpec((tm, D), lambda i, j: (i, 0)),
                  pl.BlockSpec((D, tm), lambda i, j: (0, i)),
                  pl.BlockSpec((te, D), lambda i, j: (cur(j), 0)),
                  pl.BlockSpec((D, te), lambda i, j: (0, jnp.maximum(j - 1, 0))),
                  a_spec, key_spec, a_spec, key_spec],
        out_specs=pl.BlockSpec((tm, D), lambda i, j: (i, 0)),
        out_shape=jax.ShapeDtypeStruct((T, D), F32),
        scratch_shapes=[pltpu.VMEM((D, tm), F32), pltpu.VMEM((te, tm), F32), pltpu.VMEM((te, tm), F32),
                        pltpu.VMEM((2, te, tm), BF16)],
        compiler_params=_params("parallel", "arbitrary"),
        name="peer_dense",
    )(x, xt, u, vt, na_, r2, e1, e2)


def _prep_weights(l, p):
    w_in = p["w_in"][l]
    D = w_in.shape[0]
    sbw = p["w_sb_o"].shape[1]
    mlw = p["w_ml_o"].shape[1]
    cvw = p["w_cv_o"].shape[1]
    nh = p["ml_ibias"].shape[1]
    o = 0
    w_sb = w_in[:, o:o + 3 * sbw]; o += 3 * sbw
    w_ml = w_in[:, o:o + 4 * mlw]; o += 4 * mlw
    w_if = w_in[:, o:o + 2 * nh]; o += 2 * nh
    w_ga = w_in[:, o:o + cvw]; o += cvw
    w_gb = w_in[:, o:o + cvw]; o += cvw
    w_gates = w_in[:, o:]
    w_if_pad = jnp.pad(w_if, ((0, 0), (0, LANES - 2 * nh)))
    w_if_h, w_if_l = _split2(w_if_pad)
    w_ift_h, w_ift_l = _split2(w_if.T)
    b_if = jnp.concatenate([p["ml_ibias"][l], p["ml_fbias"][l]])
    idx = jnp.arange(sbw) // SB_HD
    k1h, k1l = _split2(p["pk_k1"][l])
    k2h, k2l = _split2(p["pk_k2"][l])
    row = lambda a: a.reshape(1, -1)
    return {
        "g_mix": row(p["g_mix"][l]), "w_sb": w_sb.astype(BF16), "w_ml": w_ml.astype(BF16),
        "bd64": (idx[:, None] == idx[None, :]).astype(BF16),
        "g_sb_q": row(jnp.tile(p["g_sb_q"][l], sbw // SB_HD)),
        "g_sb_k": row(jnp.tile(p["g_sb_k"][l], sbw // SB_HD)),
        "w_if_h": w_if_h, "w_if_l": w_if_l, "b_if": row(jnp.pad(b_if, (0, LANES - 2 * nh))),
        "w_ift_h": w_ift_h, "w_ift_l": w_ift_l, "b_ift": b_if.reshape(-1, 1),
        "w_ga": w_ga.astype(BF16), "w_gb": w_gb.astype(BF16), "w_gates": w_gates.astype(BF16),
        "sb_bias": p["sb_bias"][l], "g_ml_h": p["g_ml_h"][l],
        "cv_w": p["cv_w"][l], "cv_b": row(p["cv_b"][l]),
        "cv_ln_g": row(p["cv_ln_g"][l]), "cv_ln_b": row(p["cv_ln_b"][l]),
        "w_sb_o": p["w_sb_o"][l].astype(BF16), "w_ml_o": p["w_ml_o"][l].astype(BF16),
        "w_cv_o": p["w_cv_o"][l].astype(BF16), "w_out": p["w_out"][l].astype(BF16),
        "g_xa": row(p["g_xa"][l]), "g_mem": row(p["g_mem"][l]),
        "w_xq": p["w_xq"][l].astype(BF16), "w_xk": p["w_xk"][l].astype(BF16),
        "w_xv": p["w_xv"][l].astype(BF16), "g_xq": row(p["g_xq"][l]), "g_xk": row(p["g_xk"][l]),
        "w_xo": p["w_xo"][l].astype(BF16),
        "g_ffn": row(p["g_ffn"][l]), "w_pq": p["w_pq"][l].astype(BF16),
        "pk_k1_h": k1h, "pk_k1_l": k1l, "pk_k2_h": k2h, "pk_k2_l": k2l,
        "pk_u": p["pk_u"][l].astype(BF16), "pk_vt": p["pk_v"][l].T.astype(BF16),
    }


def _tail(x, h_sb, h_ml, h_cv, lw, xa_core):
    x = _merge(x, h_sb, h_ml, h_cv, lw)
    q = _norm_proj(x, lw["g_xa"], lw["w_xq"], lw["g_xq"], True, BF16)
    x = _proj_res(x, xa_core(q), lw["w_xo"])
    xt, na, r2, e1, e2 = _peer_topk(x, lw)
    return _peer_dense(x, xt, na, r2, e1, e2, lw)


def kernel(x_prompt, x_sample, cache_sb_k, cache_sb_v, state_ml_C, state_ml_n, state_ml_m, state_conv, cache_mem_k, cache_mem_v, page_table, mem_prompt, g_mix, w_in, g_sb_q, g_sb_k, sb_bias, ml_ibias, ml_fbias, g_ml_h, cv_w, cv_b, cv_ln_g, cv_ln_b, w_sb_o, w_ml_o, w_cv_o, w_out, g_xa, g_mem, w_xq, w_xk, w_xv, g_xq, g_xk, w_xo, g_ffn, w_pq, pk_k1, pk_k2, pk_u, pk_v):
    p = dict(g_mix=g_mix, w_in=w_in, g_sb_q=g_sb_q, g_sb_k=g_sb_k, sb_bias=sb_bias, ml_ibias=ml_ibias,
             ml_fbias=ml_fbias, g_ml_h=g_ml_h, cv_w=cv_w, cv_b=cv_b, cv_ln_g=cv_ln_g, cv_ln_b=cv_ln_b,
             w_sb_o=w_sb_o, w_ml_o=w_ml_o, w_cv_o=w_cv_o, w_out=w_out, g_xa=g_xa, g_mem=g_mem, w_xq=w_xq,
             w_xk=w_xk, w_xv=w_xv, g_xq=g_xq, g_xk=g_xk, w_xo=w_xo, g_ffn=g_ffn, w_pq=w_pq,
             pk_k1=pk_k1, pk_k2=pk_k2, pk_u=pk_u, pk_v=pk_v)
    depth = w_in.shape[0]
    B, L, D = x_prompt.shape
    S = x_sample.shape[0]
    nh_sb = sb_bias.shape[1]
    nh_ml = ml_ibias.shape[1]
    sbw = w_sb_o.shape[1]
    M = mem_prompt.shape[1]
    xaw = w_xo.shape[1]
    n_pool, page = cache_sb_k.shape[1], cache_sb_k.shape[2]
    ck = cache_sb_k.transpose(0, 1, 3, 4, 2)
    cv = cache_sb_v.transpose(0, 1, 3, 4, 2)
    mem_flat = mem_prompt.reshape(B * M, D)

    yp, ys = x_prompt, x_sample.reshape(1, S, D)
    outs = [[] for _ in range(14)]
    for l in range(depth):
        lw = _prep_weights(l, p)

        mk = _norm_proj(mem_flat, lw["g_mem"], lw["w_xk"], lw["g_xk"], True, F32).reshape(B, M, xaw)
        mv = _norm_proj(mem_flat, lw["g_mem"], lw["w_xv"], lw["g_xk"], False, F32).reshape(B, M, xaw)

        q, kf, kb, vf, vb, qm, km, vm, og, gcol, grow, u = _in_proj(yp, lw)
        h_sb = _sb_prompt(q, kb, vb, lw["sb_bias"])
        h_ml, Cp, n_p, m_p = _mlstm_prompt(qm, km, vm, og, gcol, grow, lw["g_ml_h"])
        h_cv = _conv_prompt(u, lw["cv_w"], lw["cv_b"], lw["cv_ln_g"], lw["cv_ln_b"])
        xa_p = lambda qq: _xa_prompt(qq.reshape(B, L, xaw), mk, mv).reshape(B * L, xaw)
        yp = _tail(yp.reshape(B * L, D), h_sb.reshape(B * L, -1), h_ml.reshape(B * L, -1),
                   h_cv.reshape(B * L, -1), lw, xa_p).reshape(B, L, D)
        hist_p = u[:, L - (CV_K - 1):, :]
        if L < CV_K - 1:
            hist_p = jnp.concatenate([jnp.zeros((B, CV_K - 1 - L, u.shape[2]), F32), u], axis=1)

        q_s, kf_s, _, vf_s, _, qm_s, km_s, vm_s, og_s, gcol_s, _, u_s = _in_proj(ys, lw)
        h_sb_s = _sb_sample(l, page_table, q_s.astype(F32).reshape(S, nh_sb, SB_HD, 1),
                            lw["sb_bias"].reshape(nh_sb, 1), ck, cv)
        per_seq = lambda a: a.reshape(S, 1, a.shape[-1])
        h_ml_s, Cs, n_s, m_s = _mlstm_sample(
            per_seq(qm_s), per_seq(km_s), per_seq(vm_s), per_seq(og_s), per_seq(gcol_s), lw["g_ml_h"],
            state_ml_C[l], state_ml_n[l].reshape(S, nh_ml, 1, ML_HD), state_ml_m[l].reshape(S, nh_ml, 1, 1))
        xx = jnp.concatenate([state_conv[l], u_s.reshape(S, 1, -1)], axis=1)
        h_cv_s = _conv_sample(xx, lw["cv_w"], lw["cv_b"], lw["cv_ln_g"], lw["cv_ln_b"])
        xa_s = lambda qq: _xa_sample(l, qq.reshape(S, -1, XA_HD), cache_mem_k, cache_mem_v).reshape(S, xaw)
        ys = _tail(ys.reshape(S, D), h_sb_s.reshape(S, -1), h_ml_s.reshape(S, -1), h_cv_s, lw,
                   xa_s).reshape(1, S, D)

        vals = (kf.reshape(B, L, nh_sb, SB_HD), vf.reshape(B, L, nh_sb, SB_HD),
                kf_s.reshape(S, 1, nh_sb, SB_HD), vf_s.reshape(S, 1, nh_sb, SB_HD),
                Cp, n_p.reshape(B, nh_ml, ML_HD), m_p.reshape(B, nh_ml),
                Cs, n_s.reshape(S, nh_ml, ML_HD), m_s.reshape(S, nh_ml),
                hist_p, xx[:, 1:, :],
                mk.reshape(B, M, -1, XA_HD), mv.reshape(B, M, -1, XA_HD))
        for acc, v in zip(outs, vals):
            acc.append(v)
    return (yp, ys.reshape(S, 1, D)) + tuple(jnp.stack(a) for a in outs)
```
